```python
import jax
import jax.numpy as jnp
from jax import lax

D_MODEL = 4096
BATCH = 2
SEQ = 8192
DEPTH = 1

D_MIX = D_MODEL
RET_WIDTH = D_MIX // 2
LRU_WIDTH = D_MIX - RET_WIDTH
RET_HEADS = 8
RET_HEAD_DIM = RET_WIDTH // RET_HEADS
RET_CHUNK = 128
ROPE_BASE = 10000.0
LRU_BLOCKS = 16
LRU_BLOCK_DIM = LRU_WIDTH // LRU_BLOCKS
CONV_WIDTH = 4
LRU_C = 8.0
D_FF = 4 * D_MODEL
PROJ_WIDTH = 4 * RET_WIDTH + 2 * LRU_WIDTH
RMS_EPS = 1e-6
GN_EPS = 1e-5

kernel_name = "hymba_retention_rglru_sqrelu_block"


def rms_norm(x, g):
    xf = x.astype(jnp.float32)
    y = xf * lax.rsqrt(jnp.mean(xf * xf, axis=-1, keepdims=True) + RMS_EPS)
    return (y * g.astype(jnp.float32)).astype(x.dtype)


def rotary(t, pos):
    half = t.shape[-1] // 2
    inv = ROPE_BASE ** (-jnp.arange(half, dtype=jnp.float32) / half)
    ang = pos[:, None] * inv[None, :]
    cos = jnp.cos(ang)[None, :, None, :]
    sin = jnp.sin(ang)[None, :, None, :]
    t1, t2 = t[..., :half], t[..., half:]
    return jnp.concatenate([t1 * cos - t2 * sin, t2 * cos + t1 * sin], axis=-1)


def chunkwise_retention(q, k, v):
    b, s, h, d = q.shape
    c = RET_CHUNK
    n = s // c
    log_g = jnp.log1p(-jnp.exp2(-5.0 - jnp.arange(h, dtype=jnp.float32)))
    idx = jnp.arange(c, dtype=jnp.float32)
    diff = idx[:, None] - idx[None, :]
    decay_in = jnp.where(diff >= 0, jnp.exp(log_g[:, None, None] * jnp.maximum(diff, 0.0)), 0.0)
    q_dec = jnp.exp(log_g[:, None] * (idx + 1.0))[None, :, :, None]
    k_dec = jnp.exp(log_g[:, None] * (c - 1.0 - idx))[None, :, :, None]
    chunk_dec = jnp.exp(log_g * c)[None, :, None, None]

    def to_chunks(t):
        return t.reshape(b, n, c, h, d).transpose(1, 0, 3, 2, 4)

    def step(state, qkv):
        qc, kc, vc = qkv
        scores = jnp.einsum('bhnd,bhmd->bhnm', qc, kc) * decay_in
        out = (jnp.einsum('bhnm,bhme->bhne', scores, vc)
               + jnp.einsum('bhnd,bhde->bhne', qc * q_dec, state))
        state = chunk_dec * state + jnp.einsum('bhmd,bhme->bhde', kc * k_dec, vc)
        return state, out

    state0 = jnp.zeros((b, h, d, d), jnp.float32)
    _, out = lax.scan(step, state0, (to_chunks(q), to_chunks(k), to_chunks(v)))
    return out.transpose(1, 0, 3, 2, 4).reshape(b, s, h, d)


def retention_group(q, k, v, g, gn_g, pos):
    b, s, _ = q.shape
    shp = (b, s, RET_HEADS, RET_HEAD_DIM)
    qh = rotary(q.reshape(shp).astype(jnp.float32), pos)
    kh = rotary(k.reshape(shp).astype(jnp.float32), pos) * (RET_HEAD_DIM ** -0.5)
    vh = v.reshape(shp).astype(jnp.float32)
    o = chunkwise_retention(qh, kh, vh)
    mu = jnp.mean(o, axis=-1, keepdims=True)
    var = jnp.mean(jnp.square(o - mu), axis=-1, keepdims=True)
    o = ((o - mu) * lax.rsqrt(var + GN_EPS)).reshape(b, s, RET_WIDTH) * gn_g.astype(jnp.float32)
    return (jax.nn.silu(g.astype(jnp.float32)) * o).astype(q.dtype)


def rglru_group(xr, yr, conv_w, conv_b, wa, ba, wx, bx, lam, norm_g):
    b, s, w = xr.shape
    xc = lax.conv_general_dilated(
        xr, conv_w[:, None, :].astype(xr.dtype), window_strides=(1,),
        padding=[(CONV_WIDTH - 1, 0)], dimension_numbers=('NWC', 'WIO', 'NWC'),
        feature_group_count=w) + conv_b.astype(xr.dtype)
    xc = xc.astype(jnp.float32)
    xh = xc.reshape(b, s, LRU_BLOCKS, LRU_BLOCK_DIM)
    r = jax.nn.sigmoid(jnp.einsum('bsni,nij->bsnj', xh, wa.astype(jnp.float32)).reshape(b, s, w)
                       + ba.astype(jnp.float32))
    i = jax.nn.sigmoid(jnp.einsum('bsni,nij->bsnj', xh, wx.astype(jnp.float32)).reshape(b, s, w)
                       + bx.astype(jnp.float32))
    log_a = -LRU_C * r * jax.nn.softplus(-lam.astype(jnp.float32))
    a = jnp.exp(log_a)
    u = jnp.sqrt(-jnp.expm1(2.0 * log_a)) * (i * xc)

    def combine(e1, e2):
        a1, b1 = e1
        a2, b2 = e2
        return a1 * a2, a2 * b1 + b2

    _, hseq = lax.associative_scan(combine, (a, u), axis=1)
    y = hseq * jax.nn.gelu(yr.astype(jnp.float32), approximate=True)
    y = y * lax.rsqrt(jnp.mean(y * y, axis=-1, keepdims=True) + RMS_EPS) * norm_g.astype(jnp.float32)
    return y.astype(xr.dtype)


def setup_inputs(seed: int = 0) -> dict:
    key = jax.random.key(seed)
    ks = jax.random.split(key, 18)
    f32 = jnp.float32

    def nrm(k, shape, scale):
        return jax.random.normal(k, shape, f32) * scale

    a_c = jax.random.uniform(ks[10], (DEPTH, LRU_WIDTH), f32, 0.9, 0.999)
    a_base = a_c ** (1.0 / LRU_C)
    lru_lambda = jnp.log(a_base) - jnp.log1p(-a_base)
    return {
        "x": nrm(ks[0], (BATCH, SEQ, D_MODEL), 1.0),
        "norm1_g": 1.0 + nrm(ks[1], (DEPTH, D_MODEL), 0.02),
        "w_in": nrm(ks[2], (DEPTH, D_MODEL, PROJ_WIDTH), D_MODEL ** -0.5),
        "ret_gn_g": 1.0 + nrm(ks[3], (DEPTH, RET_WIDTH), 0.02),
        "conv_w": nrm(ks[4], (DEPTH, CONV_WIDTH, LRU_WIDTH), CONV_WIDTH ** -0.5),
        "conv_b": nrm(ks[5], (DEPTH, LRU_WIDTH), 0.01),
        "gate_a_w": nrm(ks[6], (DEPTH, LRU_BLOCKS, LRU_BLOCK_DIM, LRU_BLOCK_DIM), LRU_BLOCK_DIM ** -0.5),
        "gate_a_b": nrm(ks[7], (DEPTH, LRU_WIDTH), 0.01),
        "gate_x_w": nrm(ks[8], (DEPTH, LRU_BLOCKS, LRU_BLOCK_DIM, LRU_BLOCK_DIM), LRU_BLOCK_DIM ** -0.5),
        "gate_x_b": nrm(ks[9], (DEPTH, LRU_WIDTH), 0.01),
        "lru_lambda": lru_lambda,
        "lru_norm_g": 1.0 + nrm(ks[11], (DEPTH, LRU_WIDTH), 0.02),
        "w_out": nrm(ks[12], (DEPTH, D_MIX, D_MODEL), D_MIX ** -0.5),
        "norm2_g": 1.0 + nrm(ks[13], (DEPTH, D_MODEL), 0.02),
        "w_up": nrm(ks[14], (DEPTH, D_MODEL, D_FF), D_MODEL ** -0.5),
        "w_down": nrm(ks[15], (DEPTH, D_FF, D_MODEL), D_FF ** -0.5),
        "normf_g": 1.0 + nrm(ks[16], (D_MODEL,), 0.02),
    }


def reference(x, norm1_g, w_in, ret_gn_g, conv_w, conv_b, gate_a_w, gate_a_b, gate_x_w, gate_x_b,
              lru_lambda, lru_norm_g, w_out, norm2_g, w_up, w_down, normf_g):
    s = x.shape[1]
    pos = jnp.arange(s, dtype=jnp.float32)
    splits = [RET_WIDTH, 2 * RET_WIDTH, 3 * RET_WIDTH, 4 * RET_WIDTH, 4 * RET_WIDTH + LRU_WIDTH]
    for l in range(DEPTH):
        h = rms_norm(x, norm1_g[l])
        proj = h @ w_in[l].astype(h.dtype)
        q, k, v, g, xr, yr = jnp.split(proj, splits, axis=-1)
        ret = retention_group(q, k, v, g, ret_gn_g[l], pos)
        lru = rglru_group(xr, yr, conv_w[l], conv_b[l], gate_a_w[l], gate_a_b[l],
                          gate_x_w[l], gate_x_b[l], lru_lambda[l], lru_norm_g[l])
        x = x + jnp.concatenate([ret, lru], axis=-1) @ w_out[l].astype(x.dtype)
        u = rms_norm(x, norm2_g[l]) @ w_up[l].astype(x.dtype)
        x = x + jnp.square(jax.nn.relu(u)) @ w_down[l].astype(x.dtype)
    return rms_norm(x, normf_g)
```

```python
import functools

import jax
import jax.numpy as jnp
from jax import lax
from jax.experimental import pallas as pl
from jax.experimental.pallas import tpu as pltpu

RET_HEADS = 8
RET_HEAD_DIM = 256
LRU_BLOCKS = 16
LRU_BLOCK_DIM = 128
CONV_WIDTH = 4
ROPE_BASE = 10000.0
LRU_C = 8.0
RMS_EPS = 1e-6
GN_EPS = 1e-5

SUBLANES = 8
V7X_VMEM_LIMIT_BYTES = 56 * 1024 * 1024

F32 = jnp.float32
BF16 = jnp.bfloat16


def _params(semantics):
    return pltpu.CompilerParams(dimension_semantics=semantics,
                                vmem_limit_bytes=V7X_VMEM_LIMIT_BYTES)


def _rope_kernel(cos_ref, sin_ref, *, rows, half):
    r0 = pl.program_id(0) * rows
    pos = (lax.broadcasted_iota(jnp.int32, (rows, half), 0) + r0).astype(F32)
    idx = lax.broadcasted_iota(jnp.int32, (rows, half), 1).astype(F32)
    inv = jnp.exp(-(idx / half) * jnp.log(F32(ROPE_BASE)))
    ang = pos * inv
    cos_ref[...] = jnp.cos(ang)
    sin_ref[...] = jnp.sin(ang)


def _rope_table(seq, half, rows=512):
    return pl.pallas_call(
        functools.partial(_rope_kernel, rows=rows, half=half),
        grid=(seq // rows,),
        out_specs=[pl.BlockSpec((rows, half), lambda i: (i, 0))] * 2,
        out_shape=[jax.ShapeDtypeStruct((seq, half), F32)] * 2,
        compiler_params=_params(("parallel",)),
        name="rope_table",
    )()


def _norm_matmul_kernel(x_ref, g_ref, w_ref, o_ref, h_ref, *, square_relu):
    @pl.when(pl.program_id(1) == 0)
    def _():
        x = x_ref[...]
        ms = jnp.mean(x * x, axis=-1, keepdims=True)
        h_ref[...] = (x * lax.rsqrt(ms + RMS_EPS) * g_ref[...]).astype(BF16)

    acc = jnp.dot(h_ref[...], w_ref[...], preferred_element_type=F32)
    if square_relu:
        acc = jnp.square(jnp.maximum(acc, 0.0))
    o_ref[...] = acc.astype(o_ref.dtype)


def _norm_matmul(x, g, w, *, tm, tn, out_dtype, square_relu):
    t, d = x.shape
    n = w.shape[1]
    return pl.pallas_call(
        functools.partial(_norm_matmul_kernel, square_relu=square_relu),
        grid=(t // tm, n // tn),
        in_specs=[
            pl.BlockSpec((tm, d), lambda i, j: (i, 0)),
            pl.BlockSpec((1, d), lambda i, j: (0, 0)),
            pl.BlockSpec((d, tn), lambda i, j: (0, j)),
        ],
        out_specs=pl.BlockSpec((tm, tn), lambda i, j: (i, j)),
        out_shape=jax.ShapeDtypeStruct((t, n), out_dtype),
        scratch_shapes=[pltpu.VMEM((tm, d), BF16)],
        compiler_params=_params(("parallel", "arbitrary")),
        name="norm_matmul_relu2" if square_relu else "norm_matmul",
    )(x, g.reshape(1, d), w)


def _retention_kernel(q_ref, k_ref, v_ref, g_ref, cos_ref, sin_ref, gn_ref, o_ref,
                      state_ref, *, chunk, head_dim):
    head = pl.program_id(1)

    @pl.when(pl.program_id(2) == 0)
    def _():
        state_ref[...] = jnp.zeros_like(state_ref)

    half = head_dim // 2
    cos = cos_ref[...]
    sin = sin_ref[...]

    def rotary(t):
        t1, t2 = t[:, :half], t[:, half:]
        return jnp.concatenate([t1 * cos - t2 * sin, t2 * cos + t1 * sin], axis=-1)

    q = rotary(q_ref[0])
    k = rotary(k_ref[0]) * (head_dim ** -0.5)
    v = v_ref[0].astype(BF16)

    hf = jnp.full((1, 1), head, jnp.int32).astype(F32)
    log_g = jnp.log1p(-jnp.exp2(-5.0 - hf))
    row = lax.broadcasted_iota(jnp.int32, (chunk, chunk), 0)
    col = lax.broadcasted_iota(jnp.int32, (chunk, chunk), 1)
    diff = (row - col).astype(F32)
    decay_in = jnp.where(diff >= 0, jnp.exp(log_g * jnp.maximum(diff, 0.0)), 0.0)
    idx = lax.broadcasted_iota(jnp.int32, (chunk, 1), 0).astype(F32)
    q_dec = jnp.exp(log_g * (idx + 1.0))
    k_dec = jnp.exp(log_g * (chunk - 1.0 - idx))
    chunk_dec = jnp.exp(log_g * chunk)

    qb = q.astype(BF16)
    kb = k.astype(BF16)
    scores = lax.dot_general(qb, kb, (((1,), (1,)), ((), ())),
                             preferred_element_type=F32) * decay_in
    state = state_ref[...]
    out = (jnp.dot(scores.astype(BF16), v, preferred_element_type=F32)
           + jnp.dot((q * q_dec).astype(BF16), state.astype(BF16),
                     preferred_element_type=F32))
    kv = lax.dot_general((k * k_dec).astype(BF16), v, (((0,), (0,)), ((), ())),
                         preferred_element_type=F32)
    state_ref[...] = chunk_dec * state + kv

    mu = jnp.mean(out, axis=-1, keepdims=True)
    cen = out - mu
    var = jnp.mean(cen * cen, axis=-1, keepdims=True)
    o = cen * lax.rsqrt(var + GN_EPS) * gn_ref[...]
    o_ref[0] = (jax.nn.silu(g_ref[0]) * o).astype(o_ref.dtype)


def _retention(proj, cos, sin, gn_g, *, chunk):
    b, s, _ = proj.shape
    h, d = RET_HEADS, RET_HEAD_DIM

    def col(off):
        return pl.BlockSpec((1, chunk, d), lambda bi, hi, ci: (bi, ci, off + hi))

    return pl.pallas_call(
        functools.partial(_retention_kernel, chunk=chunk, head_dim=d),
        grid=(b, h, s // chunk),
        in_specs=[
            col(0), col(h), col(2 * h), col(3 * h),
            pl.BlockSpec((chunk, d // 2), lambda bi, hi, ci: (ci, 0)),
            pl.BlockSpec((chunk, d // 2), lambda bi, hi, ci: (ci, 0)),
            pl.BlockSpec((1, d), lambda bi, hi, ci: (0, hi)),
        ],
        out_specs=pl.BlockSpec((1, chunk, d), lambda bi, hi, ci: (bi, ci, hi)),
        out_shape=jax.ShapeDtypeStruct((b, s, h * d), BF16),
        scratch_shapes=[pltpu.VMEM((d, d), F32)],
        compiler_params=_params(("parallel", "parallel", "arbitrary")),
        name="retention",
    )(proj, proj, proj, proj, cos, sin, gn_g.reshape(1, h * d))


def _rglru_kernel(xr_ref, yr_ref, cw_ref, cb_ref, wg_ref, ba_ref, bx_ref, lam_ref, ng_ref,
                  o_ref, ext_ref, a_ref, u_ref, h_ref, *, tc):
    @pl.when(pl.program_id(1) == 0)
    def _():
        ext_ref[0:SUBLANES, :] = jnp.zeros((SUBLANES, ext_ref.shape[1]), F32)
        h_ref[...] = jnp.zeros_like(h_ref)

    ext_ref[SUBLANES:SUBLANES + tc, :] = xr_ref[0]
    xc = cb_ref[...]
    for j in range(CONV_WIDTH):
        lo = SUBLANES - (CONV_WIDTH - 1) + j
        xc = xc + cw_ref[j:j + 1, :] * ext_ref[lo:lo + tc, :]
    ext_ref[0:SUBLANES, :] = ext_ref[tc:tc + SUBLANES, :]

    sp = jax.nn.softplus(-lam_ref[...])
    for n in range(LRU_BLOCKS):
        sl = slice(n * LRU_BLOCK_DIM, (n + 1) * LRU_BLOCK_DIM)
        xb = xc[:, sl]
        gates = jnp.dot(xb.astype(BF16), wg_ref[n], preferred_element_type=F32)
        r = jax.nn.sigmoid(gates[:, :LRU_BLOCK_DIM] + ba_ref[:, sl])
        i = jax.nn.sigmoid(gates[:, LRU_BLOCK_DIM:] + bx_ref[:, sl])
        log_a = -LRU_C * r * sp[:, sl]
        a = jnp.exp(log_a)
        a_ref[:, sl] = a
        one_minus_a2 = -jnp.tanh(log_a) * (1.0 + a * a)
        u_ref[:, sl] = jnp.sqrt(one_minus_a2) * (i * xb)

    def step(t, h):
        h = a_ref[pl.ds(t, 1), :] * h + u_ref[pl.ds(t, 1), :]
        u_ref[pl.ds(t, 1), :] = h
        return h

    h_ref[...] = lax.fori_loop(0, tc, step, h_ref[...], unroll=8)

    y = u_ref[...] * jax.nn.gelu(yr_ref[0], approximate=True)
    ms = jnp.mean(y * y, axis=-1, keepdims=True)
    o_ref[0] = (y * lax.rsqrt(ms + RMS_EPS) * ng_ref[...]).astype(o_ref.dtype)


def _rglru(proj, conv_w, conv_b, wg, ba, bx, lam, ng, *, tc):
    b, s, pw = proj.shape
    w = LRU_BLOCKS * LRU_BLOCK_DIM
    xr_blk = (pw - 2 * w) // w
    row = lambda a: a.reshape(1, w)
    vec = pl.BlockSpec((1, w), lambda bi, ti: (0, 0))
    return pl.pallas_call(
        functools.partial(_rglru_kernel, tc=tc),
        grid=(b, s // tc),
        in_specs=[
            pl.BlockSpec((1, tc, w), lambda bi, ti: (bi, ti, xr_blk)),
            pl.BlockSpec((1, tc, w), lambda bi, ti: (bi, ti, xr_blk + 1)),
            pl.BlockSpec((CONV_WIDTH, w), lambda bi, ti: (0, 0)),
            vec,
            pl.BlockSpec((LRU_BLOCKS, LRU_BLOCK_DIM, 2 * LRU_BLOCK_DIM),
                         lambda bi, ti: (0, 0, 0)),
            vec, vec, vec, vec,
        ],
        out_specs=pl.BlockSpec((1, tc, w), lambda bi, ti: (bi, ti, 0)),
        out_shape=jax.ShapeDtypeStruct((b, s, w), BF16),
        scratch_shapes=[
            pltpu.VMEM((tc + SUBLANES, w), F32),
            pltpu.VMEM((tc, w), F32),
            pltpu.VMEM((tc, w), F32),
            pltpu.VMEM((1, w), F32),
        ],
        compiler_params=_params(("parallel", "arbitrary")),
        name="rglru",
    )(proj, proj, conv_w, row(conv_b), wg, row(ba), row(bx), row(lam), row(ng))


def _out_proj_kernel(ret_ref, lru_ref, wr_ref, wl_ref, x_ref, o_ref):
    acc = jnp.dot(ret_ref[...], wr_ref[...], preferred_element_type=F32)
    acc = acc + jnp.dot(lru_ref[...], wl_ref[...], preferred_element_type=F32)
    o_ref[...] = x_ref[...] + acc


def _out_proj(ret, lru, w_out, x, *, tm, tn):
    t, d = x.shape
    kr, kl = ret.shape[1], lru.shape[1]
    return pl.pallas_call(
        _out_proj_kernel,
        grid=(t // tm, d // tn),
        in_specs=[
            pl.BlockSpec((tm, kr), lambda i, j: (i, 0)),
            pl.BlockSpec((tm, kl), lambda i, j: (i, 0)),
            pl.BlockSpec((kr, tn), lambda i, j: (0, j)),
            pl.BlockSpec((kl, tn), lambda i, j: (kr // kl, j)),
            pl.BlockSpec((tm, tn), lambda i, j: (i, j)),
        ],
        out_specs=pl.BlockSpec((tm, tn), lambda i, j: (i, j)),
        out_shape=jax.ShapeDtypeStruct((t, d), F32),
        compiler_params=_params(("parallel", "arbitrary")),
        name="out_proj",
    )(ret, lru, w_out, w_out, x)


def _down_norm_kernel(a_ref, w_ref, x_ref, g_ref, o_ref, acc_ref):
    kk = pl.program_id(1)

    @pl.when(kk == 0)
    def _():
        acc_ref[...] = x_ref[...]

    acc_ref[...] += jnp.dot(a_ref[...], w_ref[...], preferred_element_type=F32)

    @pl.when(kk == pl.num_programs(1) - 1)
    def _():
        x = acc_ref[...]
        ms = jnp.mean(x * x, axis=-1, keepdims=True)
        o_ref[...] = x * lax.rsqrt(ms + RMS_EPS) * g_ref[...]


def _down_norm(act, w_down, x, g, *, tm, tk):
    t, d = x.shape
    f = act.shape[1]
    return pl.pallas_call(
        _down_norm_kernel,
        grid=(t // tm, f // tk),
        in_specs=[
            pl.BlockSpec((tm, tk), lambda i, k: (i, k)),
            pl.BlockSpec((tk, d), lambda i, k: (k, 0)),
            pl.BlockSpec((tm, d), lambda i, k: (i, 0), pipeline_mode=pl.Buffered(1)),
            pl.BlockSpec((1, d), lambda i, k: (0, 0)),
        ],
        out_specs=pl.BlockSpec((tm, d), lambda i, k: (i, 0)),
        out_shape=jax.ShapeDtypeStruct((t, d), F32),
        scratch_shapes=[pltpu.VMEM((tm, d), F32)],
        compiler_params=_params(("parallel", "arbitrary")),
        name="down_norm",
    )(act, w_down, x, g.reshape(1, d))


def kernel(x, norm1_g, w_in, ret_gn_g, conv_w, conv_b, gate_a_w, gate_a_b, gate_x_w, gate_x_b,
           lru_lambda, lru_norm_g, w_out, norm2_g, w_up, w_down, normf_g):
    b, s, d = x.shape
    depth = w_in.shape[0]
    cos, sin = _rope_table(s, RET_HEAD_DIM // 2)
    xt = x.reshape(b * s, d)
    for l in range(depth):
        proj = _norm_matmul(xt, norm1_g[l], w_in[l].astype(BF16), tm=512, tn=1024,
                            out_dtype=F32, square_relu=False)
        proj = proj.reshape(b, s, -1)
        ret = _retention(proj, cos, sin, ret_gn_g[l], chunk=256)
        wg = jnp.concatenate([gate_a_w[l], gate_x_w[l]], axis=-1).astype(BF16)
        lru = _rglru(proj, conv_w[l], conv_b[l], wg, gate_a_b[l], gate_x_b[l],
                     lru_lambda[l], lru_norm_g[l], tc=256)
        xt = _out_proj(ret.reshape(b * s, -1), lru.reshape(b * s, -1),
                       w_out[l].astype(BF16), xt, tm=1024, tn=1024)
        act = _norm_matmul(xt, norm2_g[l], w_up[l].astype(BF16), tm=512, tn=1024,
                           out_dtype=BF16, square_relu=True)
        last = l == depth - 1
        if last:
            xt = _down_norm(act, w_down[l].astype(BF16), xt, normf_g, tm=512, tk=512)
        else:
            xt = _out_proj(act[:, :act.shape[1] // 2], act[:, act.shape[1] // 2:],
                           w_down[l].astype(BF16), xt, tm=1024, tn=1024)
    return xt.reshape(b, s, d)
```

```python
import functools
import math

import jax
import jax.numpy as jnp
from jax import lax
from jax.experimental import pallas as pl
from jax.experimental.pallas import tpu as pltpu

RET_HEADS = 8
RET_HEAD_DIM = 256
LRU_BLOCKS = 16
LRU_BLOCK_DIM = 128
CONV_WIDTH = 4
ROPE_BASE = 10000.0
LRU_C = 8.0
RMS_EPS = 1e-6
GN_EPS = 1e-5

SUBLANES = 8
LANES = 128
V7X_VMEM_LIMIT_BYTES = 56 * 1024 * 1024

F32 = jnp.float32
BF16 = jnp.bfloat16


def _params(semantics):
    return pltpu.CompilerParams(dimension_semantics=semantics,
                                vmem_limit_bytes=V7X_VMEM_LIMIT_BYTES)


def _rope_kernel(cos_ref, sin_ref, *, rows, half):
    r0 = pl.program_id(0) * rows
    pos = (lax.broadcasted_iota(jnp.int32, (rows, half), 0) + r0).astype(F32)
    idx = lax.broadcasted_iota(jnp.int32, (rows, half), 1).astype(F32)
    inv = jnp.exp(-(idx / half) * jnp.log(F32(ROPE_BASE)))
    ang = pos * inv
    cos_ref[...] = jnp.cos(ang)
    sin_ref[...] = jnp.sin(ang)


def _rope_table(seq, half, rows=512):
    return pl.pallas_call(
        functools.partial(_rope_kernel, rows=rows, half=half),
        grid=(seq // rows,),
        out_specs=[pl.BlockSpec((rows, half), lambda i: (i, 0))] * 2,
        out_shape=[jax.ShapeDtypeStruct((seq, half), F32)] * 2,
        compiler_params=_params(("parallel",)),
        name="rope_table",
    )()


def _norm_matmul_kernel(x_ref, g_ref, w_ref, o_ref, h_ref):
    @pl.when(pl.program_id(1) == 0)
    def _():
        x = x_ref[...]
        ms = jnp.mean(x * x, axis=-1, keepdims=True)
        h_ref[...] = (x * lax.rsqrt(ms + RMS_EPS) * g_ref[...]).astype(BF16)

    o_ref[...] = jnp.dot(h_ref[...], w_ref[...], preferred_element_type=F32)


def _norm_matmul(x, g, w, *, tm, tn):
    t, d = x.shape
    n = w.shape[1]
    return pl.pallas_call(
        _norm_matmul_kernel,
        grid=(t // tm, n // tn),
        in_specs=[
            pl.BlockSpec((tm, d), lambda i, j: (i, 0)),
            pl.BlockSpec((1, d), lambda i, j: (0, 0)),
            pl.BlockSpec((d, tn), lambda i, j: (0, j)),
        ],
        out_specs=pl.BlockSpec((tm, tn), lambda i, j: (i, j)),
        out_shape=jax.ShapeDtypeStruct((t, n), F32),
        scratch_shapes=[pltpu.VMEM((tm, d), BF16)],
        compiler_params=_params(("parallel", "arbitrary")),
        name="norm_matmul",
    )(x, g.reshape(1, d), w)


def _retention_kernel(q_ref, k_ref, v_ref, g_ref, cos_ref, sin_ref, gn_ref, o_ref,
                      state_ref, dec_ref, qd_ref, kd_ref, cd_ref, *, chunk, n_sub, head_dim):
    scale = head_dim ** -0.5
    assert math.frexp(scale)[0] == 0.5

    @pl.when(pl.program_id(2) == 0)
    def _():
        state_ref[...] = jnp.zeros_like(state_ref)
        hf = jnp.full((1, 1), pl.program_id(1), jnp.int32).astype(F32)
        log_g = jnp.log1p(-jnp.exp2(-5.0 - hf))
        row = lax.broadcasted_iota(jnp.int32, (chunk, chunk), 0)
        col = lax.broadcasted_iota(jnp.int32, (chunk, chunk), 1)
        diff = (row - col).astype(F32)
        dec_ref[...] = jnp.where(diff >= 0, jnp.exp(log_g * jnp.maximum(diff, 0.0)), 0.0) * scale
        idx = lax.broadcasted_iota(jnp.int32, (chunk, head_dim), 0).astype(F32)
        qd_ref[...] = jnp.exp(log_g * (idx + 1.0))
        kd_ref[...] = jnp.exp(log_g * (chunk - 1.0 - idx)) * scale
        cd_ref[...] = jnp.broadcast_to(jnp.exp(log_g * chunk), cd_ref.shape)

    half = head_dim // 2

    def rotary(t, cos, sin):
        t1, t2 = t[:, :half], t[:, half:]
        return jnp.concatenate([t1 * cos - t2 * sin, t2 * cos + t1 * sin], axis=-1)

    state = state_ref[...]
    for c in range(n_sub):
        rows = pl.ds(c * chunk, chunk)
        cos = cos_ref[rows, :]
        sin = sin_ref[rows, :]
        q = rotary(q_ref[0, rows, :], cos, sin)
        k = rotary(k_ref[0, rows, :], cos, sin)
        qb = q.astype(BF16)
        vb = v_ref[0, rows, :].astype(BF16)
        scores = lax.dot_general(qb, k.astype(BF16), (((1,), (1,)), ((), ())),
                                 preferred_element_type=F32) * dec_ref[...]
        out = (jnp.dot(scores.astype(BF16), vb, preferred_element_type=F32)
               + qd_ref[...] * jnp.dot(qb, state.astype(BF16), preferred_element_type=F32))
        kv = lax.dot_general((k * kd_ref[...]).astype(BF16), vb, (((0,), (0,)), ((), ())),
                             preferred_element_type=F32)
        state = cd_ref[...] * state + kv

        mu = jnp.mean(out, axis=-1, keepdims=True)
        cen = out - mu
        var = jnp.mean(cen * cen, axis=-1, keepdims=True)
        o = cen * lax.rsqrt(var + GN_EPS) * gn_ref[...]
        o_ref[0, rows, :] = (jax.nn.silu(g_ref[0, rows, :]) * o).astype(o_ref.dtype)
    state_ref[...] = state


def _retention(proj, cos, sin, gn_g, *, chunk, rows):
    b, s, _ = proj.shape
    h, d = RET_HEADS, RET_HEAD_DIM

    def col(off):
        return pl.BlockSpec((1, rows, d), lambda bi, hi, ci: (bi, ci, off + hi))

    return pl.pallas_call(
        functools.partial(_retention_kernel, chunk=chunk, n_sub=rows // chunk, head_dim=d),
        grid=(b, h, s // rows),
        in_specs=[
            col(0), col(h), col(2 * h), col(3 * h),
            pl.BlockSpec((rows, d // 2), lambda bi, hi, ci: (ci, 0)),
            pl.BlockSpec((rows, d // 2), lambda bi, hi, ci: (ci, 0)),
            pl.BlockSpec((1, d), lambda bi, hi, ci: (0, hi)),
        ],
        out_specs=pl.BlockSpec((1, rows, d), lambda bi, hi, ci: (bi, ci, hi)),
        out_shape=jax.ShapeDtypeStruct((b, s, h * d), BF16),
        scratch_shapes=[
            pltpu.VMEM((d, d), F32),
            pltpu.VMEM((chunk, chunk), F32),
            pltpu.VMEM((chunk, d), F32),
            pltpu.VMEM((chunk, d), F32),
            pltpu.VMEM((1, d), F32),
        ],
        compiler_params=_params(("parallel", "parallel", "arbitrary")),
        name="retention",
    )(proj, proj, proj, proj, cos, sin, gn_g.reshape(1, h * d))


def _rglru_kernel(xr_ref, yr_ref, cw_ref, cb_ref, wg_ref, ba_ref, bx_ref, lam_ref, ng_ref,
                  o_ref, ext_ref, a_ref, u_ref, h_ref, *, tc):
    @pl.when(pl.program_id(1) == 0)
    def _():
        ext_ref[0:SUBLANES, :] = jnp.zeros((SUBLANES, ext_ref.shape[1]), F32)
        h_ref[...] = jnp.zeros_like(h_ref)

    ext_ref[SUBLANES:SUBLANES + tc, :] = xr_ref[0]
    xc = cb_ref[...]
    for j in range(CONV_WIDTH):
        lo = SUBLANES - (CONV_WIDTH - 1) + j
        xc = xc + cw_ref[j:j + 1, :] * ext_ref[lo:lo + tc, :]
    ext_ref[0:SUBLANES, :] = ext_ref[tc:tc + SUBLANES, :]

    sp = jax.nn.softplus(-lam_ref[...])
    for n in range(LRU_BLOCKS):
        sl = slice(n * LRU_BLOCK_DIM, (n + 1) * LRU_BLOCK_DIM)
        xb = xc[:, sl]
        gates = jnp.dot(xb.astype(BF16), wg_ref[n], preferred_element_type=F32)
        r = jax.nn.sigmoid(gates[:, :LRU_BLOCK_DIM] + ba_ref[:, sl])
        i = jax.nn.sigmoid(gates[:, LRU_BLOCK_DIM:] + bx_ref[:, sl])
        log_a = -LRU_C * r * sp[:, sl]
        a = jnp.exp(log_a)
        a_ref[:, sl] = a
        one_minus_a2 = -jnp.tanh(log_a) * (1.0 + a * a)
        u_ref[:, sl] = jnp.sqrt(one_minus_a2) * (i * xb)

    def step(t, h):
        h = a_ref[pl.ds(t, 1), :] * h + u_ref[pl.ds(t, 1), :]
        u_ref[pl.ds(t, 1), :] = h
        return h

    h_ref[...] = lax.fori_loop(0, tc, step, h_ref[...], unroll=8)

    y = u_ref[...] * jax.nn.gelu(yr_ref[0], approximate=True)
    ms = jnp.mean(y * y, axis=-1, keepdims=True)
    o_ref[0] = (y * lax.rsqrt(ms + RMS_EPS) * ng_ref[...]).astype(o_ref.dtype)


def _rglru(proj, conv_w, conv_b, wg, ba, bx, lam, ng, *, tc):
    b, s, pw = proj.shape
    w = LRU_BLOCKS * LRU_BLOCK_DIM
    xr_blk = (pw - 2 * w) // w
    row = lambda a: a.reshape(1, w)
    vec = pl.BlockSpec((1, w), lambda bi, ti: (0, 0))
    return pl.pallas_call(
        functools.partial(_rglru_kernel, tc=tc),
        grid=(b, s // tc),
        in_specs=[
            pl.BlockSpec((1, tc, w), lambda bi, ti: (bi, ti, xr_blk)),
            pl.BlockSpec((1, tc, w), lambda bi, ti: (bi, ti, xr_blk + 1)),
            pl.BlockSpec((CONV_WIDTH, w), lambda bi, ti: (0, 0)),
            vec,
            pl.BlockSpec((LRU_BLOCKS, LRU_BLOCK_DIM, 2 * LRU_BLOCK_DIM),
                         lambda bi, ti: (0, 0, 0)),
            vec, vec, vec, vec,
        ],
        out_specs=pl.BlockSpec((1, tc, w), lambda bi, ti: (bi, ti, 0)),
        out_shape=jax.ShapeDtypeStruct((b, s, w), BF16),
        scratch_shapes=[
            pltpu.VMEM((tc + SUBLANES, w), F32),
            pltpu.VMEM((tc, w), F32),
            pltpu.VMEM((tc, w), F32),
            pltpu.VMEM((1, w), F32),
        ],
        compiler_params=_params(("parallel", "arbitrary")),
        name="rglru",
    )(proj, proj, conv_w, row(conv_b), wg, row(ba), row(bx), row(lam), row(ng))


def _out_proj_kernel(ret_ref, lru_ref, wr_ref, wl_ref, x_ref, g_ref, o_ref, xg_ref, ssq_ref):
    acc = jnp.dot(ret_ref[...], wr_ref[...], preferred_element_type=F32)
    acc = acc + jnp.dot(lru_ref[...], wl_ref[...], preferred_element_type=F32)
    x1 = x_ref[...] + acc
    o_ref[...] = x1
    xg_ref[...] = (x1 * g_ref[...]).astype(BF16)
    sq = x1 * x1
    part = sq[:, :LANES]
    for c in range(1, sq.shape[1] // LANES):
        part = part + sq[:, c * LANES:(c + 1) * LANES]

    @pl.when(pl.program_id(1) == 0)
    def _():
        ssq_ref[...] = part

    @pl.when(pl.program_id(1) != 0)
    def _():
        ssq_ref[...] += part


def _out_proj(ret, lru, w_out, x, g, *, tm, tn):
    t, d = x.shape
    kr, kl = ret.shape[1], lru.shape[1]
    return pl.pallas_call(
        _out_proj_kernel,
        grid=(t // tm, d // tn),
        in_specs=[
            pl.BlockSpec((tm, kr), lambda i, j: (i, 0)),
            pl.BlockSpec((tm, kl), lambda i, j: (i, 0)),
            pl.BlockSpec((kr, tn), lambda i, j: (0, j)),
            pl.BlockSpec((kl, tn), lambda i, j: (kr // kl, j)),
            pl.BlockSpec((tm, tn), lambda i, j: (i, j)),
            pl.BlockSpec((1, tn), lambda i, j: (0, j)),
        ],
        out_specs=[
            pl.BlockSpec((tm, tn), lambda i, j: (i, j)),
            pl.BlockSpec((tm, tn), lambda i, j: (i, j)),
            pl.BlockSpec((tm, LANES), lambda i, j: (i, 0)),
        ],
        out_shape=[
            jax.ShapeDtypeStruct((t, d), F32),
            jax.ShapeDtypeStruct((t, d), BF16),
            jax.ShapeDtypeStruct((t, LANES), F32),
        ],
        compiler_params=_params(("parallel", "arbitrary")),
        name="out_proj",
    )(ret, lru, w_out, w_out, x, g.reshape(1, d))


def _up_kernel(xg_ref, ssq_ref, w_ref, o_ref, *, d_model):
    ms = jnp.sum(ssq_ref[...], axis=-1, keepdims=True) / d_model
    r = lax.rsqrt(ms + RMS_EPS)
    acc = jnp.dot(xg_ref[...], w_ref[...], preferred_element_type=F32)
    o_ref[...] = jnp.square(jnp.maximum(acc * r, 0.0)).astype(o_ref.dtype)


def _up(xg, ssq, w, *, tm, tn):
    t, d = xg.shape
    n = w.shape[1]
    return pl.pallas_call(
        functools.partial(_up_kernel, d_model=d),
        grid=(t // tm, n // tn),
        in_specs=[
            pl.BlockSpec((tm, d), lambda i, j: (i, 0)),
            pl.BlockSpec((tm, LANES), lambda i, j: (i, 0)),
            pl.BlockSpec((d, tn), lambda i, j: (0, j)),
        ],
        out_specs=pl.BlockSpec((tm, tn), lambda i, j: (i, j)),
        out_shape=jax.ShapeDtypeStruct((t, n), BF16),
        compiler_params=_params(("parallel", "arbitrary")),
        name="mlp_up",
    )(xg, ssq, w)


def _down_norm_kernel(a_ref, w_ref, x_ref, g_ref, o_ref):
    kk = pl.program_id(1)

    @pl.when(kk == 0)
    def _():
        o_ref[...] = x_ref[...]

    o_ref[...] += jnp.dot(a_ref[...], w_ref[...], preferred_element_type=F32)

    @pl.when(kk == pl.num_programs(1) - 1)
    def _():
        x = o_ref[...]
        ms = jnp.mean(x * x, axis=-1, keepdims=True)
        o_ref[...] = x * lax.rsqrt(ms + RMS_EPS) * g_ref[...]


def _down_norm(act, w_down, x, g, *, tm, tk):
    t, d = x.shape
    f = act.shape[1]
    return pl.pallas_call(
        _down_norm_kernel,
        grid=(t // tm, f // tk),
        in_specs=[
            pl.BlockSpec((tm, tk), lambda i, k: (i, k)),
            pl.BlockSpec((tk, d), lambda i, k: (k, 0)),
            pl.BlockSpec((tm, d), lambda i, k: (i, 0), pipeline_mode=pl.Buffered(1)),
            pl.BlockSpec((1, d), lambda i, k: (0, 0)),
        ],
        out_specs=pl.BlockSpec((tm, d), lambda i, k: (i, 0)),
        out_shape=jax.ShapeDtypeStruct((t, d), F32),
        compiler_params=_params(("parallel", "arbitrary")),
        name="down_norm",
    )(act, w_down, x, g.reshape(1, d))


def kernel(x, norm1_g, w_in, ret_gn_g, conv_w, conv_b, gate_a_w, gate_a_b, gate_x_w, gate_x_b,
           lru_lambda, lru_norm_g, w_out, norm2_g, w_up, w_down, normf_g):
    b, s, d = x.shape
    assert w_in.shape[0] == 1, "single-layer problem"
    cos, sin = _rope_table(s, RET_HEAD_DIM // 2)
    xt = x.reshape(b * s, d)
    proj = _norm_matmul(xt, norm1_g[0], w_in[0].astype(BF16), tm=512, tn=1024)
    proj = proj.reshape(b, s, -1)
    ret = _retention(proj, cos, sin, ret_gn_g[0], chunk=256, rows=1024)
    wg = jnp.concatenate([gate_a_w[0], gate_x_w[0]], axis=-1).astype(BF16)
    lru = _rglru(proj, conv_w[0], conv_b[0], wg, gate_a_b[0], gate_x_b[0],
                 lru_lambda[0], lru_norm_g[0], tc=256)
    x1, xg, ssq = _out_proj(ret.reshape(b * s, -1), lru.reshape(b * s, -1),
                            w_out[0].astype(BF16), xt, norm2_g[0], tm=1024, tn=512)
    act = _up(xg, ssq, w_up[0].astype(BF16), tm=1024, tn=1024)
    out = _down_norm(act, w_down[0].astype(BF16), x1, normf_g, tm=512, tk=1024)
    return out.reshape(b, s, d)
```

```python
import functools
import math

import jax
import jax.numpy as jnp
from jax import lax
from jax.experimental import pallas as pl
from jax.experimental.pallas import tpu as pltpu

RET_HEADS = 8
RET_HEAD_DIM = 256
LRU_BLOCKS = 16
LRU_BLOCK_DIM = 128
CONV_WIDTH = 4
ROPE_BASE = 10000.0
LRU_C = 8.0
RMS_EPS = 1e-6
GN_EPS = 1e-5

SUBLANES = 8
LANES = 128
V7X_VMEM_LIMIT_BYTES = 60 * 1024 * 1024
NORM_ROWS = 4 * SUBLANES

F32 = jnp.float32
BF16 = jnp.bfloat16


def _params(semantics):
    return pltpu.CompilerParams(dimension_semantics=semantics,
                                vmem_limit_bytes=V7X_VMEM_LIMIT_BYTES)


def _rope_kernel(cos_ref, sin_ref, *, rows, half):
    r0 = pl.program_id(0) * rows
    pos = (lax.broadcasted_iota(jnp.int32, (rows, half), 0) + r0).astype(F32)
    idx = lax.broadcasted_iota(jnp.int32, (rows, half), 1).astype(F32)
    inv = jnp.exp(-(idx / half) * jnp.log(F32(ROPE_BASE)))
    ang = pos * inv
    cos_ref[...] = jnp.cos(ang)
    sin_ref[...] = jnp.sin(ang)


def _rope_table(seq, half, rows=512):
    return pl.pallas_call(
        functools.partial(_rope_kernel, rows=rows, half=half),
        grid=(seq // rows,),
        out_specs=[pl.BlockSpec((rows, half), lambda i: (i, 0))] * 2,
        out_shape=[jax.ShapeDtypeStruct((seq, half), F32)] * 2,
        compiler_params=_params(("parallel",)),
        name="rope_table",
    )()


def _norm_matmul_kernel(x_ref, g_ref, w_ref, o_ref, h_ref):
    @pl.when(pl.program_id(1) == 0)
    def _():
        x = x_ref[...]
        ms = jnp.mean(x * x, axis=-1, keepdims=True)
        h_ref[...] = (x * lax.rsqrt(ms + RMS_EPS) * g_ref[...]).astype(BF16)

    o_ref[...] = jnp.dot(h_ref[...], w_ref[...], preferred_element_type=F32)


def _norm_matmul(x, g, w, *, tm, tn):
    t, d = x.shape
    n = w.shape[1]
    return pl.pallas_call(
        _norm_matmul_kernel,
        grid=(t // tm, n // tn),
        in_specs=[
            pl.BlockSpec((tm, d), lambda i, j: (i, 0)),
            pl.BlockSpec((1, d), lambda i, j: (0, 0)),
            pl.BlockSpec((d, tn), lambda i, j: (0, j)),
        ],
        out_specs=pl.BlockSpec((tm, tn), lambda i, j: (i, j)),
        out_shape=jax.ShapeDtypeStruct((t, n), F32),
        scratch_shapes=[pltpu.VMEM((tm, d), BF16)],
        compiler_params=_params(("parallel", "arbitrary")),
        name="norm_matmul",
    )(x, g.reshape(1, d), w)


def _retention_kernel(q_ref, k_ref, v_ref, g_ref, cos_ref, sin_ref, gn_ref, o_ref,
                      state_ref, dec_ref, qd_ref, kd_ref, cd_ref, *, chunk, n_sub, head_dim):
    scale = head_dim ** -0.5
    assert math.frexp(scale)[0] == 0.5

    @pl.when(pl.program_id(2) == 0)
    def _():
        state_ref[...] = jnp.zeros_like(state_ref)
        hf = jnp.full((1, 1), pl.program_id(1), jnp.int32).astype(F32)
        log_g = jnp.log1p(-jnp.exp2(-5.0 - hf))
        row = lax.broadcasted_iota(jnp.int32, (chunk, chunk), 0)
        col = lax.broadcasted_iota(jnp.int32, (chunk, chunk), 1)
        diff = (row - col).astype(F32)
        dec_ref[...] = jnp.where(diff >= 0, jnp.exp(log_g * jnp.maximum(diff, 0.0)), 0.0) * scale
        idx = lax.broadcasted_iota(jnp.int32, (chunk, head_dim), 0).astype(F32)
        qd_ref[...] = jnp.exp(log_g * (idx + 1.0))
        kd_ref[...] = jnp.exp(log_g * (chunk - 1.0 - idx)) * scale
        cd_ref[...] = jnp.broadcast_to(jnp.exp(log_g * chunk), cd_ref.shape)

    half = head_dim // 2

    def rotary(t, cos, sin):
        t1, t2 = t[:, :half], t[:, half:]
        return jnp.concatenate([t1 * cos - t2 * sin, t2 * cos + t1 * sin], axis=-1)

    state = state_ref[...]
    for c in range(n_sub):
        rows = pl.ds(c * chunk, chunk)
        cos = cos_ref[rows, :]
        sin = sin_ref[rows, :]
        q = rotary(q_ref[0, rows, :], cos, sin)
        k = rotary(k_ref[0, rows, :], cos, sin)
        qb = q.astype(BF16)
        vb = v_ref[0, rows, :].astype(BF16)
        scores = lax.dot_general(qb, k.astype(BF16), (((1,), (1,)), ((), ())),
                                 preferred_element_type=F32) * dec_ref[...]
        out = (jnp.dot(scores.astype(BF16), vb, preferred_element_type=F32)
               + qd_ref[...] * jnp.dot(qb, state.astype(BF16), preferred_element_type=F32))
        kv = lax.dot_general((k * kd_ref[...]).astype(BF16), vb, (((0,), (0,)), ((), ())),
                             preferred_element_type=F32)
        state = cd_ref[...] * state + kv

        mu = jnp.mean(out, axis=-1, keepdims=True)
        cen = out - mu
        var = jnp.mean(cen * cen, axis=-1, keepdims=True)
        o = cen * lax.rsqrt(var + GN_EPS) * gn_ref[...]
        o_ref[0, rows, :] = (jax.nn.silu(g_ref[0, rows, :]) * o).astype(o_ref.dtype)
    state_ref[...] = state


def _retention(proj, cos, sin, gn_g, *, chunk, rows):
    b, s, _ = proj.shape
    h, d = RET_HEADS, RET_HEAD_DIM

    def col(off):
        return pl.BlockSpec((1, rows, d), lambda bi, hi, ci: (bi, ci, off + hi))

    return pl.pallas_call(
        functools.partial(_retention_kernel, chunk=chunk, n_sub=rows // chunk, head_dim=d),
        grid=(b, h, s // rows),
        in_specs=[
            col(0), col(h), col(2 * h), col(3 * h),
            pl.BlockSpec((rows, d // 2), lambda bi, hi, ci: (ci, 0)),
            pl.BlockSpec((rows, d // 2), lambda bi, hi, ci: (ci, 0)),
            pl.BlockSpec((1, d), lambda bi, hi, ci: (0, hi)),
        ],
        out_specs=pl.BlockSpec((1, rows, d), lambda bi, hi, ci: (bi, ci, hi)),
        out_shape=jax.ShapeDtypeStruct((b, s, h * d), BF16),
        scratch_shapes=[
            pltpu.VMEM((d, d), F32),
            pltpu.VMEM((chunk, chunk), F32),
            pltpu.VMEM((chunk, d), F32),
            pltpu.VMEM((chunk, d), F32),
            pltpu.VMEM((1, d), F32),
        ],
        compiler_params=_params(("parallel", "parallel", "arbitrary")),
        name="retention",
    )(proj, proj, proj, proj, cos, sin, gn_g.reshape(1, h * d))


def _rglru_kernel(xr_ref, yr_ref, cw_ref, cb_ref, wg_ref, ba_ref, bx_ref, lam_ref, ng_ref,
                  o_ref, ext_ref, a_ref, u_ref, h_ref, *, tc):
    @pl.when(pl.program_id(1) == 0)
    def _():
        ext_ref[0:SUBLANES, :] = jnp.zeros((SUBLANES, ext_ref.shape[1]), F32)
        h_ref[...] = jnp.zeros_like(h_ref)

    ext_ref[SUBLANES:SUBLANES + tc, :] = xr_ref[0]
    xe = ext_ref[...]
    acc = cw_ref[0:1, :] * xe
    for j in range(1, CONV_WIDTH):
        acc = cw_ref[j:j + 1, :] * xe + pltpu.roll(acc, 1, axis=0)
    xc = acc[SUBLANES:, :] + cb_ref[...]
    ext_ref[0:SUBLANES, :] = ext_ref[tc:tc + SUBLANES, :]

    sp = jax.nn.softplus(-lam_ref[...])
    for n in range(LRU_BLOCKS):
        sl = slice(n * LRU_BLOCK_DIM, (n + 1) * LRU_BLOCK_DIM)
        xb = xc[:, sl]
        gates = jnp.dot(xb.astype(BF16), wg_ref[n], preferred_element_type=F32)
        r = jax.nn.sigmoid(gates[:, :LRU_BLOCK_DIM] + ba_ref[:, sl])
        i = jax.nn.sigmoid(gates[:, LRU_BLOCK_DIM:] + bx_ref[:, sl])
        log_a = -LRU_C * r * sp[:, sl]
        a = jnp.exp(log_a)
        a_ref[:, sl] = a
        one_minus_a2 = -jnp.tanh(log_a) * (1.0 + a * a)
        u_ref[:, sl] = jnp.sqrt(one_minus_a2) * (i * xb)

    def step(t, h):
        h = a_ref[pl.ds(t, 1), :] * h + u_ref[pl.ds(t, 1), :]
        u_ref[pl.ds(t, 1), :] = h
        return h

    h_ref[...] = lax.fori_loop(0, tc, step, h_ref[...], unroll=8)

    y = u_ref[...] * jax.nn.gelu(yr_ref[0], approximate=True)
    ms = jnp.mean(y * y, axis=-1, keepdims=True)
    o_ref[0] = (y * lax.rsqrt(ms + RMS_EPS) * ng_ref[...]).astype(o_ref.dtype)


def _rglru(proj, conv_w, conv_b, wg, ba, bx, lam, ng, *, tc):
    b, s, pw = proj.shape
    w = LRU_BLOCKS * LRU_BLOCK_DIM
    xr_blk = (pw - 2 * w) // w
    row = lambda a: a.reshape(1, w)
    vec = pl.BlockSpec((1, w), lambda bi, ti: (0, 0))
    return pl.pallas_call(
        functools.partial(_rglru_kernel, tc=tc),
        grid=(b, s // tc),
        in_specs=[
            pl.BlockSpec((1, tc, w), lambda bi, ti: (bi, ti, xr_blk)),
            pl.BlockSpec((1, tc, w), lambda bi, ti: (bi, ti, xr_blk + 1)),
            pl.BlockSpec((CONV_WIDTH, w), lambda bi, ti: (0, 0)),
            vec,
            pl.BlockSpec((LRU_BLOCKS, LRU_BLOCK_DIM, 2 * LRU_BLOCK_DIM),
                         lambda bi, ti: (0, 0, 0)),
            vec, vec, vec, vec,
        ],
        out_specs=pl.BlockSpec((1, tc, w), lambda bi, ti: (bi, ti, 0)),
        out_shape=jax.ShapeDtypeStruct((b, s, w), BF16),
        scratch_shapes=[
            pltpu.VMEM((tc + SUBLANES, w), F32),
            pltpu.VMEM((tc, w), F32),
            pltpu.VMEM((tc, w), F32),
            pltpu.VMEM((1, w), F32),
        ],
        compiler_params=_params(("parallel", "arbitrary")),
        name="rglru",
    )(proj, proj, conv_w, row(conv_b), wg, row(ba), row(bx), row(lam), row(ng))


def _out_proj_kernel(ret_ref, lru_ref, wr_ref, wl_ref, x_ref, g_ref, wn_ref,
                     o_ref, xg_ref, ssq_ref, wnb_ref):
    @pl.when(pl.program_id(1) == 0)
    def _():
        ssq_ref[...] = jnp.zeros_like(ssq_ref)

    acc = jnp.dot(ret_ref[...], wr_ref[...], preferred_element_type=F32)
    acc = acc + jnp.dot(lru_ref[...], wl_ref[...], preferred_element_type=F32)
    x1 = x_ref[...] + acc
    o_ref[...] = x1
    xg_ref[...] = (x1 * g_ref[...]).astype(BF16)
    sq = x1 * x1
    part = sq[:, :LANES]
    for c in range(1, sq.shape[1] // LANES):
        part = part + sq[:, c * LANES:(c + 1) * LANES]
    ssq_ref[...] += part
    wnb_ref[...] = wn_ref[...].astype(BF16)


def _slab_rows(w_rows, steps):
    rows, rem = divmod(w_rows, steps)
    assert rem == 0 and rows % (2 * SUBLANES) == 0, (w_rows, steps)
    return rows


def _out_proj(ret, lru, w_out, x, g, w_next, *, tm, tn):
    t, d = x.shape
    kr, kl = ret.shape[1], lru.shape[1]
    ni, nj = t // tm, d // tn
    wn_rows, wn_cols = w_next.shape
    sr = _slab_rows(wn_rows, ni * nj)
    slab = pl.BlockSpec((sr, wn_cols), lambda i, j: (i * nj + j, 0))
    return pl.pallas_call(
        _out_proj_kernel,
        grid=(ni, nj),
        in_specs=[
            pl.BlockSpec((tm, kr), lambda i, j: (i, 0)),
            pl.BlockSpec((tm, kl), lambda i, j: (i, 0)),
            pl.BlockSpec((kr, tn), lambda i, j: (0, j)),
            pl.BlockSpec((kl, tn), lambda i, j: (kr // kl, j)),
            pl.BlockSpec((tm, tn), lambda i, j: (i, j)),
            pl.BlockSpec((1, tn), lambda i, j: (0, j)),
            slab,
        ],
        out_specs=[
            pl.BlockSpec((tm, tn), lambda i, j: (i, j)),
            pl.BlockSpec((tm, tn), lambda i, j: (i, j)),
            pl.BlockSpec((tm, LANES), lambda i, j: (i, 0)),
            slab,
        ],
        out_shape=[
            jax.ShapeDtypeStruct((t, d), F32),
            jax.ShapeDtypeStruct((t, d), BF16),
            jax.ShapeDtypeStruct((t, LANES), F32),
            jax.ShapeDtypeStruct((wn_rows, wn_cols), BF16),
        ],
        compiler_params=_params(("parallel", "arbitrary")),
        name="out_proj",
    )(ret, lru, w_out, w_out, x, g.reshape(1, d), w_next)


def _up_kernel(xg_ref, ssq_ref, w_ref, wn_ref, o_ref, wnb_ref, *, d_model):
    ms = jnp.sum(ssq_ref[...], axis=-1, keepdims=True) / d_model
    r = lax.rsqrt(ms + RMS_EPS)
    acc = jnp.dot(xg_ref[...], w_ref[...], preferred_element_type=F32)
    o_ref[...] = jnp.square(jnp.maximum(acc * r, 0.0)).astype(o_ref.dtype)
    wnb_ref[...] = wn_ref[...].astype(BF16)


def _up(xg, ssq, w, w_next, *, tm, tn):
    t, d = xg.shape
    n = w.shape[1]
    ni, nj = t // tm, n // tn
    wn_rows, wn_cols = w_next.shape
    sr = _slab_rows(wn_rows, ni * nj)
    slab = pl.BlockSpec((sr, wn_cols), lambda i, j: (i * nj + j, 0))
    return pl.pallas_call(
        functools.partial(_up_kernel, d_model=d),
        grid=(ni, nj),
        in_specs=[
            pl.BlockSpec((tm, d), lambda i, j: (i, 0)),
            pl.BlockSpec((tm, LANES), lambda i, j: (i, 0)),
            pl.BlockSpec((d, tn), lambda i, j: (0, j)),
            slab,
        ],
        out_specs=[pl.BlockSpec((tm, tn), lambda i, j: (i, j)), slab],
        out_shape=[jax.ShapeDtypeStruct((t, n), BF16),
                   jax.ShapeDtypeStruct((wn_rows, wn_cols), BF16)],
        compiler_params=_params(("parallel", "arbitrary")),
        name="mlp_up",
    )(xg, ssq, w, w_next)


def _down_norm_kernel(a_ref, w_ref, x_ref, g_ref, o_ref, r_ref, *, tn):
    kk = pl.program_id(1)
    d = o_ref.shape[1]
    xs = x_ref.shape[1]

    def panels(accumulate):
        for c in range(d // tn):
            cols = slice(c * tn, (c + 1) * tn)
            part = jnp.dot(a_ref[...], w_ref[:, cols], preferred_element_type=F32)
            o_ref[:, cols] = o_ref[:, cols] + part if accumulate else part

    pl.when(kk == 0)(functools.partial(panels, False))
    pl.when(kk != 0)(functools.partial(panels, True))

    lanes = pl.ds(pl.multiple_of(kk * xs, xs), xs)
    o_ref[:, lanes] += x_ref[...]

    @pl.when(kk == pl.num_programs(1) - 1)
    def _():
        def row_block(r):
            return pl.ds(pl.multiple_of(r * NORM_ROWS, NORM_ROWS), NORM_ROWS)

        def row_scale(r, carry):
            x = o_ref[row_block(r), :]
            ms = jnp.mean(x * x, axis=-1, keepdims=True)
            r_ref[row_block(r), :] = lax.rsqrt(ms + RMS_EPS)
            return carry

        def apply_scale(r, carry):
            rows = row_block(r)
            o_ref[rows, :] = o_ref[rows, :] * r_ref[rows, :] * g_ref[...]
            return carry

        n_blocks = o_ref.shape[0] // NORM_ROWS
        lax.fori_loop(0, n_blocks, row_scale, 0, unroll=4)
        lax.fori_loop(0, n_blocks, apply_scale, 0, unroll=4)


def _down_norm(act, w_down, x, g, *, tm, tk, tn):
    t, d = x.shape
    f = act.shape[1]
    nk = f // tk
    assert d % nk == 0 and (d // nk) % LANES == 0
    return pl.pallas_call(
        functools.partial(_down_norm_kernel, tn=tn),
        grid=(t // tm, nk),
        in_specs=[
            pl.BlockSpec((tm, tk), lambda i, k: (i, k)),
            pl.BlockSpec((tk, d), lambda i, k: (k, 0)),
            pl.BlockSpec((tm, d // nk), lambda i, k: (i, k)),
            pl.BlockSpec((1, d), lambda i, k: (0, 0)),
        ],
        out_specs=pl.BlockSpec((tm, d), lambda i, k: (i, 0)),
        out_shape=jax.ShapeDtypeStruct((t, d), F32),
        scratch_shapes=[pltpu.VMEM((tm, 1), F32)],
        compiler_params=_params(("parallel", "arbitrary")),
        name="down_norm",
    )(act, w_down, x, g.reshape(1, d))


def kernel(x, norm1_g, w_in, ret_gn_g, conv_w, conv_b, gate_a_w, gate_a_b, gate_x_w, gate_x_b,
           lru_lambda, lru_norm_g, w_out, norm2_g, w_up, w_down, normf_g):
    b, s, d = x.shape
    assert w_in.shape[0] == 1, "single-layer problem"
    cos, sin = _rope_table(s, RET_HEAD_DIM // 2)
    xt = x.reshape(b * s, d)
    proj = _norm_matmul(xt, norm1_g[0], w_in[0].astype(BF16), tm=512, tn=1024)
    proj = proj.reshape(b, s, -1)
    ret = _retention(proj, cos, sin, ret_gn_g[0], chunk=256, rows=1024)
    wg = jnp.concatenate([gate_a_w[0], gate_x_w[0]], axis=-1).astype(BF16)
    lru = _rglru(proj, conv_w[0], conv_b[0], wg, gate_a_b[0], gate_x_b[0],
                 lru_lambda[0], lru_norm_g[0], tc=256)
    x1, xg, ssq, w_up_b = _out_proj(ret.reshape(b * s, -1), lru.reshape(b * s, -1),
                                    w_out[0].astype(BF16), xt, norm2_g[0], w_up[0],
                                    tm=1024, tn=512)
    act, w_down_b = _up(xg, ssq, w_up_b, w_down[0], tm=1024, tn=1024)
    out = _down_norm(act, w_down_b, x1, normf_g, tm=1024, tk=1024, tn=1024)
    return out.reshape(b, s, d)
```

```python
import functools
import math

import jax
import jax.numpy as jnp
from jax import lax
from jax.experimental import pallas as pl
from jax.experimental.pallas import tpu as pltpu

RET_HEADS = 8
RET_HEAD_DIM = 256
LRU_BLOCKS = 16
LRU_BLOCK_DIM = 128
CONV_WIDTH = 4
ROPE_BASE = 10000.0
LRU_C = 8.0
RMS_EPS = 1e-6
GN_EPS = 1e-5

SUBLANES = 8
LANES = 128
V7X_VMEM_LIMIT_BYTES = 60 * 1024 * 1024
NORM_ROWS = 4 * SUBLANES

F32 = jnp.float32
BF16 = jnp.bfloat16


def _params(semantics):
    return pltpu.CompilerParams(dimension_semantics=semantics,
                                vmem_limit_bytes=V7X_VMEM_LIMIT_BYTES)


def _rope_kernel(cos_ref, sin_ref, *, rows, half):
    r0 = pl.program_id(0) * rows
    pos = (lax.broadcasted_iota(jnp.int32, (rows, half), 0) + r0).astype(F32)
    idx = lax.broadcasted_iota(jnp.int32, (rows, half), 1).astype(F32)
    inv = jnp.exp(-(idx / half) * jnp.log(F32(ROPE_BASE)))
    ang = pos * inv
    cos_ref[...] = jnp.cos(ang)
    sin_ref[...] = jnp.sin(ang)


def _rope_table(seq, half, rows=512):
    return pl.pallas_call(
        functools.partial(_rope_kernel, rows=rows, half=half),
        grid=(seq // rows,),
        out_specs=[pl.BlockSpec((rows, half), lambda i: (i, 0))] * 2,
        out_shape=[jax.ShapeDtypeStruct((seq, half), F32)] * 2,
        compiler_params=_params(("parallel",)),
        name="rope_table",
    )()


def _rmsnorm_kernel(x_ref, g_ref, h_ref):
    x = x_ref[...]
    ms = jnp.mean(x * x, axis=-1, keepdims=True)
    h_ref[...] = (x * lax.rsqrt(ms + RMS_EPS) * g_ref[...]).astype(h_ref.dtype)


def _rmsnorm(x, g, *, tm):
    t, d = x.shape
    return pl.pallas_call(
        _rmsnorm_kernel,
        grid=(t // tm,),
        in_specs=[pl.BlockSpec((tm, d), lambda i: (i, 0)),
                  pl.BlockSpec((1, d), lambda i: (0, 0))],
        out_specs=pl.BlockSpec((tm, d), lambda i: (i, 0)),
        out_shape=jax.ShapeDtypeStruct((t, d), BF16),
        compiler_params=_params(("parallel",)),
        name="rmsnorm",
    )(x, g.reshape(1, d))


def _in_proj_kernel(h_ref, w_ref, cos_ref, sin_ref, qkv_ref, rest_ref, *,
                    n_rot, n_bf16, head_dim):
    j = pl.program_id(1)
    half = head_dim // 2

    def dot():
        return jnp.dot(h_ref[...], w_ref[...], preferred_element_type=F32)

    @pl.when(j < n_rot)
    def _():
        acc = dot()
        cos = cos_ref[...]
        sin = sin_ref[...]
        for hd in range(acc.shape[1] // head_dim):
            lo = slice(hd * head_dim, hd * head_dim + half)
            hi = slice(hd * head_dim + half, (hd + 1) * head_dim)
            t1, t2 = acc[:, lo], acc[:, hi]
            qkv_ref[:, lo] = (t1 * cos - t2 * sin).astype(qkv_ref.dtype)
            qkv_ref[:, hi] = (t2 * cos + t1 * sin).astype(qkv_ref.dtype)

    @pl.when(jnp.logical_and(j >= n_rot, j < n_bf16))
    def _():
        qkv_ref[...] = dot().astype(qkv_ref.dtype)

    @pl.when(j >= n_bf16)
    def _():
        rest_ref[...] = dot()


def _in_proj(h, w, cos, sin, *, tm, tn, rot_width, bf16_width, head_dim):
    t, d = h.shape
    n = w.shape[1]
    seq = cos.shape[0]
    assert rot_width % tn == 0 and bf16_width % tn == 0 and tn % head_dim == 0
    assert seq % tm == 0
    n_rot, n_bf16 = rot_width // tn, bf16_width // tn
    pos_blocks = seq // tm
    return pl.pallas_call(
        functools.partial(_in_proj_kernel, n_rot=n_rot, n_bf16=n_bf16, head_dim=head_dim),
        grid=(t // tm, n // tn),
        in_specs=[
            pl.BlockSpec((tm, d), lambda i, j: (i, 0)),
            pl.BlockSpec((d, tn), lambda i, j: (0, j)),
            pl.BlockSpec((tm, head_dim // 2), lambda i, j: (i % pos_blocks, 0)),
            pl.BlockSpec((tm, head_dim // 2), lambda i, j: (i % pos_blocks, 0)),
        ],
        out_specs=[
            pl.BlockSpec((tm, tn), lambda i, j: (i, jnp.minimum(j, n_bf16 - 1))),
            pl.BlockSpec((tm, tn), lambda i, j: (i, jnp.maximum(j - n_bf16, 0))),
        ],
        out_shape=[jax.ShapeDtypeStruct((t, bf16_width), BF16),
                   jax.ShapeDtypeStruct((t, n - bf16_width), F32)],
        compiler_params=_params(("parallel", "arbitrary")),
        name="in_proj",
    )(h, w, cos, sin)


def _slab_rows(w_rows, steps):
    rows, rem = divmod(w_rows, steps)
    assert rem == 0 and rows % (2 * SUBLANES) == 0, (w_rows, steps)
    return rows


def _retention_kernel(q_ref, k_ref, v_ref, g_ref, gn_ref, wn_ref, o_ref, wnb_ref,
                      state_ref, dec_ref, qd_ref, kd_ref, cd_ref, *, chunk, n_sub, head_dim):
    scale = head_dim ** -0.5
    assert math.frexp(scale)[0] == 0.5

    @pl.when(pl.program_id(2) == 0)
    def _():
        state_ref[...] = jnp.zeros_like(state_ref)
        hf = jnp.full((1, 1), pl.program_id(1), jnp.int32).astype(F32)
        log_g = jnp.log1p(-jnp.exp2(-5.0 - hf))
        row = lax.broadcasted_iota(jnp.int32, (chunk, chunk), 0)
        col = lax.broadcasted_iota(jnp.int32, (chunk, chunk), 1)
        diff = (row - col).astype(F32)
        dec_ref[...] = jnp.where(diff >= 0, jnp.exp(log_g * jnp.maximum(diff, 0.0)), 0.0) * scale
        idx = lax.broadcasted_iota(jnp.int32, (chunk, head_dim), 0).astype(F32)
        qd_ref[...] = jnp.exp(log_g * (idx + 1.0))
        kd_ref[...] = jnp.exp(log_g * (chunk - 1.0 - idx)) * scale
        cd_ref[...] = jnp.broadcast_to(jnp.exp(log_g * chunk), cd_ref.shape)

    wnb_ref[...] = wn_ref[...].astype(BF16)

    state = state_ref[...]
    for c in range(n_sub):
        rows = pl.ds(c * chunk, chunk)
        qb = q_ref[0, rows, :]
        kb = k_ref[0, rows, :]
        vb = v_ref[0, rows, :]
        scores = lax.dot_general(qb, kb, (((1,), (1,)), ((), ())),
                                 preferred_element_type=F32) * dec_ref[...]
        out = (jnp.dot(scores.astype(BF16), vb, preferred_element_type=F32)
               + qd_ref[...] * jnp.dot(qb, state.astype(BF16), preferred_element_type=F32))
        kd = (kb.astype(F32) * kd_ref[...]).astype(BF16)
        kv = lax.dot_general(kd, vb, (((0,), (0,)), ((), ())), preferred_element_type=F32)
        state = cd_ref[...] * state + kv

        mu = jnp.mean(out, axis=-1, keepdims=True)
        cen = out - mu
        var = jnp.mean(cen * cen, axis=-1, keepdims=True)
        o = cen * lax.rsqrt(var + GN_EPS) * gn_ref[...]
        o_ref[0, rows, :] = (jax.nn.silu(g_ref[0, rows, :]) * o).astype(o_ref.dtype)
    state_ref[...] = state


def _retention(qkv, rest, gn_g, w_next, *, chunk, rows):
    b, s, _ = qkv.shape
    h, d = RET_HEADS, RET_HEAD_DIM
    nc = s // rows
    wn_rows, wn_cols = w_next.shape
    sr = _slab_rows(wn_rows, b * h * nc)
    slab = pl.BlockSpec((sr, wn_cols), lambda bi, hi, ci: ((bi * h + hi) * nc + ci, 0))

    def col(off):
        return pl.BlockSpec((1, rows, d), lambda bi, hi, ci: (bi, ci, off + hi))

    return pl.pallas_call(
        functools.partial(_retention_kernel, chunk=chunk, n_sub=rows // chunk, head_dim=d),
        grid=(b, h, nc),
        in_specs=[
            col(0), col(h), col(2 * h), col(0),
            pl.BlockSpec((1, d), lambda bi, hi, ci: (0, hi)),
            slab,
        ],
        out_specs=[pl.BlockSpec((1, rows, d), lambda bi, hi, ci: (bi, ci, hi)), slab],
        out_shape=[jax.ShapeDtypeStruct((b, s, h * d), BF16),
                   jax.ShapeDtypeStruct((wn_rows, wn_cols), BF16)],
        scratch_shapes=[
            pltpu.VMEM((d, d), F32),
            pltpu.VMEM((chunk, chunk), F32),
            pltpu.VMEM((chunk, d), F32),
            pltpu.VMEM((chunk, d), F32),
            pltpu.VMEM((1, d), F32),
        ],
        compiler_params=_params(("parallel", "parallel", "arbitrary")),
        name="retention",
    )(qkv, qkv, qkv, rest, gn_g.reshape(1, h * d), w_next)


def _rglru_kernel(xr_ref, yr_ref, cw_ref, cb_ref, wg_ref, ba_ref, bx_ref, lam_ref, ng_ref,
                  o_ref, ext_ref, a_ref, u_ref, h_ref, *, tc):
    @pl.when(pl.program_id(1) == 0)
    def _():
        ext_ref[0:SUBLANES, :] = jnp.zeros((SUBLANES, ext_ref.shape[1]), F32)
        h_ref[...] = jnp.zeros_like(h_ref)

    ext_ref[SUBLANES:SUBLANES + tc, :] = xr_ref[0]
    xe = ext_ref[...]
    acc = cw_ref[0:1, :] * xe
    for j in range(1, CONV_WIDTH):
        acc = cw_ref[j:j + 1, :] * xe + pltpu.roll(acc, 1, axis=0)
    xc = acc[SUBLANES:, :] + cb_ref[...]
    ext_ref[0:SUBLANES, :] = ext_ref[tc:tc + SUBLANES, :]

    sp = jax.nn.softplus(-lam_ref[...])
    for n in range(LRU_BLOCKS):
        sl = slice(n * LRU_BLOCK_DIM, (n + 1) * LRU_BLOCK_DIM)
        xb = xc[:, sl]
        gates = jnp.dot(xb.astype(BF16), wg_ref[n], preferred_element_type=F32)
        r = jax.nn.sigmoid(gates[:, :LRU_BLOCK_DIM] + ba_ref[:, sl])
        i = jax.nn.sigmoid(gates[:, LRU_BLOCK_DIM:] + bx_ref[:, sl])
        log_a = -LRU_C * r * sp[:, sl]
        a = jnp.exp(log_a)
        a_ref[:, sl] = a
        one_minus_a2 = -jnp.tanh(log_a) * (1.0 + a * a)
        u_ref[:, sl] = jnp.sqrt(one_minus_a2) * (i * xb)

    def step(t, h):
        h = a_ref[pl.ds(t, 1), :] * h + u_ref[pl.ds(t, 1), :]
        u_ref[pl.ds(t, 1), :] = h
        return h

    h_ref[...] = lax.fori_loop(0, tc, step, h_ref[...], unroll=8)

    y = u_ref[...] * jax.nn.gelu(yr_ref[0], approximate=True)
    ms = jnp.mean(y * y, axis=-1, keepdims=True)
    o_ref[0] = (y * lax.rsqrt(ms + RMS_EPS) * ng_ref[...]).astype(o_ref.dtype)


def _rglru(proj, conv_w, conv_b, wg, ba, bx, lam, ng, *, tc):
    b, s, pw = proj.shape
    w = LRU_BLOCKS * LRU_BLOCK_DIM
    xr_blk = (pw - 2 * w) // w
    row = lambda a: a.reshape(1, w)
    vec = pl.BlockSpec((1, w), lambda bi, ti: (0, 0))
    return pl.pallas_call(
        functools.partial(_rglru_kernel, tc=tc),
        grid=(b, s // tc),
        in_specs=[
            pl.BlockSpec((1, tc, w), lambda bi, ti: (bi, ti, xr_blk)),
            pl.BlockSpec((1, tc, w), lambda bi, ti: (bi, ti, xr_blk + 1)),
            pl.BlockSpec((CONV_WIDTH, w), lambda bi, ti: (0, 0)),
            vec,
            pl.BlockSpec((LRU_BLOCKS, LRU_BLOCK_DIM, 2 * LRU_BLOCK_DIM),
                         lambda bi, ti: (0, 0, 0)),
            vec, vec, vec, vec,
        ],
        out_specs=pl.BlockSpec((1, tc, w), lambda bi, ti: (bi, ti, 0)),
        out_shape=jax.ShapeDtypeStruct((b, s, w), BF16),
        scratch_shapes=[
            pltpu.VMEM((tc + SUBLANES, w), F32),
            pltpu.VMEM((tc, w), F32),
            pltpu.VMEM((tc, w), F32),
            pltpu.VMEM((1, w), F32),
        ],
        compiler_params=_params(("parallel", "arbitrary")),
        name="rglru",
    )(proj, proj, conv_w, row(conv_b), wg, row(ba), row(bx), row(lam), row(ng))


def _out_proj_kernel(ret_ref, lru_ref, wr_ref, wl_ref, x_ref, g_ref, wn_ref,
                     o_ref, xg_ref, ssq_ref, wnb_ref):
    @pl.when(pl.program_id(1) == 0)
    def _():
        ssq_ref[...] = jnp.zeros_like(ssq_ref)

    acc = jnp.dot(ret_ref[...], wr_ref[...], preferred_element_type=F32)
    acc = acc + jnp.dot(lru_ref[...], wl_ref[...], preferred_element_type=F32)
    x1 = x_ref[...] + acc
    o_ref[...] = x1
    xg_ref[...] = (x1 * g_ref[...]).astype(BF16)
    sq = x1 * x1
    part = sq[:, :LANES]
    for c in range(1, sq.shape[1] // LANES):
        part = part + sq[:, c * LANES:(c + 1) * LANES]
    ssq_ref[...] += part
    wnb_ref[...] = wn_ref[...].astype(BF16)


def _out_proj(ret, lru, w_out, x, g, w_next, *, tm, tn):
    t, d = x.shape
    kr, kl = ret.shape[1], lru.shape[1]
    ni, nj = t // tm, d // tn
    wn_rows, wn_cols = w_next.shape
    sr = _slab_rows(wn_rows, ni * nj)
    slab = pl.BlockSpec((sr, wn_cols), lambda i, j: (i * nj + j, 0))
    return pl.pallas_call(
        _out_proj_kernel,
        grid=(ni, nj),
        in_specs=[
            pl.BlockSpec((tm, kr), lambda i, j: (i, 0)),
            pl.BlockSpec((tm, kl), lambda i, j: (i, 0)),
            pl.BlockSpec((kr, tn), lambda i, j: (0, j)),
            pl.BlockSpec((kl, tn), lambda i, j: (kr // kl, j)),
            pl.BlockSpec((tm, tn), lambda i, j: (i, j)),
            pl.BlockSpec((1, tn), lambda i, j: (0, j)),
            slab,
        ],
        out_specs=[
            pl.BlockSpec((tm, tn), lambda i, j: (i, j)),
            pl.BlockSpec((tm, tn), lambda i, j: (i, j)),
            pl.BlockSpec((tm, LANES), lambda i, j: (i, 0)),
            slab,
        ],
        out_shape=[
            jax.ShapeDtypeStruct((t, d), F32),
            jax.ShapeDtypeStruct((t, d), BF16),
            jax.ShapeDtypeStruct((t, LANES), F32),
            jax.ShapeDtypeStruct((wn_rows, wn_cols), BF16),
        ],
        compiler_params=_params(("parallel", "arbitrary")),
        name="out_proj",
    )(ret, lru, w_out, w_out, x, g.reshape(1, d), w_next)


def _up_kernel(xg_ref, ssq_ref, w_ref, wn_ref, o_ref, wnb_ref, *, d_model):
    ms = jnp.sum(ssq_ref[...], axis=-1, keepdims=True) / d_model
    r = lax.rsqrt(ms + RMS_EPS)
    acc = jnp.dot(xg_ref[...], w_ref[...], preferred_element_type=F32)
    o_ref[...] = jnp.square(jnp.maximum(acc * r, 0.0)).astype(o_ref.dtype)
    wnb_ref[...] = wn_ref[...].astype(BF16)


def _up(xg, ssq, w, w_next, *, tm, tn):
    t, d = xg.shape
    n = w.shape[1]
    ni, nj = t // tm, n // tn
    wn_rows, wn_cols = w_next.shape
    sr = _slab_rows(wn_rows, ni * nj)
    slab = pl.BlockSpec((sr, wn_cols), lambda i, j: (i * nj + j, 0))
    return pl.pallas_call(
        functools.partial(_up_kernel, d_model=d),
        grid=(ni, nj),
        in_specs=[
            pl.BlockSpec((tm, d), lambda i, j: (i, 0)),
            pl.BlockSpec((tm, LANES), lambda i, j: (i, 0)),
            pl.BlockSpec((d, tn), lambda i, j: (0, j)),
            slab,
        ],
        out_specs=[pl.BlockSpec((tm, tn), lambda i, j: (i, j)), slab],
        out_shape=[jax.ShapeDtypeStruct((t, n), BF16),
                   jax.ShapeDtypeStruct((wn_rows, wn_cols), BF16)],
        compiler_params=_params(("parallel", "arbitrary")),
        name="mlp_up",
    )(xg, ssq, w, w_next)


def _down_norm_kernel(a_ref, w_ref, x_ref, g_ref, o_ref, r_ref, *, tn):
    kk = pl.program_id(1)
    d = o_ref.shape[1]
    xs = x_ref.shape[1]

    def panels(accumulate):
        for c in range(d // tn):
            cols = slice(c * tn, (c + 1) * tn)
            part = jnp.dot(a_ref[...], w_ref[:, cols], preferred_element_type=F32)
            o_ref[:, cols] = o_ref[:, cols] + part if accumulate else part

    pl.when(kk == 0)(functools.partial(panels, False))
    pl.when(kk != 0)(functools.partial(panels, True))

    lanes = pl.ds(pl.multiple_of(kk * xs, xs), xs)
    o_ref[:, lanes] += x_ref[...]

    @pl.when(kk == pl.num_programs(1) - 1)
    def _():
        def row_block(r):
            return pl.ds(pl.multiple_of(r * NORM_ROWS, NORM_ROWS), NORM_ROWS)

        def row_scale(r, carry):
            x = o_ref[row_block(r), :]
            ms = jnp.mean(x * x, axis=-1, keepdims=True)
            r_ref[row_block(r), :] = lax.rsqrt(ms + RMS_EPS)
            return carry

        def apply_scale(r, carry):
            rows = row_block(r)
            o_ref[rows, :] = o_ref[rows, :] * r_ref[rows, :] * g_ref[...]
            return carry

        n_blocks = o_ref.shape[0] // NORM_ROWS
        lax.fori_loop(0, n_blocks, row_scale, 0, unroll=4)
        lax.fori_loop(0, n_blocks, apply_scale, 0, unroll=4)


def _down_norm(act, w_down, x, g, *, tm, tk, tn):
    t, d = x.shape
    f = act.shape[1]
    nk = f // tk
    assert d % nk == 0 and (d // nk) % LANES == 0
    return pl.pallas_call(
        functools.partial(_down_norm_kernel, tn=tn),
        grid=(t // tm, nk),
        in_specs=[
            pl.BlockSpec((tm, tk), lambda i, k: (i, k)),
            pl.BlockSpec((tk, d), lambda i, k: (k, 0)),
            pl.BlockSpec((tm, d // nk), lambda i, k: (i, k)),
            pl.BlockSpec((1, d), lambda i, k: (0, 0)),
        ],
        out_specs=pl.BlockSpec((tm, d), lambda i, k: (i, 0)),
        out_shape=jax.ShapeDtypeStruct((t, d), F32),
        scratch_shapes=[pltpu.VMEM((tm, 1), F32)],
        compiler_params=_params(("parallel", "arbitrary")),
        name="down_norm",
    )(act, w_down, x, g.reshape(1, d))


def kernel(x, norm1_g, w_in, ret_gn_g, conv_w, conv_b, gate_a_w, gate_a_b, gate_x_w, gate_x_b,
           lru_lambda, lru_norm_g, w_out, norm2_g, w_up, w_down, normf_g):
    b, s, d = x.shape
    assert w_in.shape[0] == 1, "single-layer problem"
    cos, sin = _rope_table(s, RET_HEAD_DIM // 2)
    xt = x.reshape(b * s, d)
    ret_width = RET_HEADS * RET_HEAD_DIM
    h = _rmsnorm(xt, norm1_g[0], tm=256)
    qkv, rest = _in_proj(h, w_in[0].astype(BF16), cos, sin, tm=1024, tn=1024,
                         rot_width=2 * ret_width, bf16_width=3 * ret_width,
                         head_dim=RET_HEAD_DIM)
    qkv = qkv.reshape(b, s, -1)
    rest = rest.reshape(b, s, -1)
    ret, w_out_b = _retention(qkv, rest, ret_gn_g[0], w_out[0], chunk=256, rows=1024)
    wg = jnp.concatenate([gate_a_w[0], gate_x_w[0]], axis=-1).astype(BF16)
    lru = _rglru(rest, conv_w[0], conv_b[0], wg, gate_a_b[0], gate_x_b[0],
                 lru_lambda[0], lru_norm_g[0], tc=256)
    x1, xg, ssq, w_up_b = _out_proj(ret.reshape(b * s, -1), lru.reshape(b * s, -1),
                                    w_out_b, xt, norm2_g[0], w_up[0], tm=1024, tn=512)
    act, w_down_b = _up(xg, ssq, w_up_b, w_down[0], tm=1024, tn=1024)
    out = _down_norm(act, w_down_b, x1, normf_g, tm=1024, tk=1024, tn=1024)
    return out.reshape(b, s, d)
```

```python
import functools
import math

import jax
import jax.numpy as jnp
from jax import lax
from jax.experimental import pallas as pl
from jax.experimental.pallas import tpu as pltpu

RET_HEADS = 8
RET_HEAD_DIM = 256
LRU_BLOCKS = 16
LRU_BLOCK_DIM = 128
CONV_WIDTH = 4
ROPE_BASE = 10000.0
LRU_C = 8.0
RMS_EPS = 1e-6
GN_EPS = 1e-5

SUBLANES = 8
LANES = 128
V7X_VMEM_LIMIT_BYTES = 60 * 1024 * 1024
NORM_ROWS = 4 * SUBLANES

F32 = jnp.float32
BF16 = jnp.bfloat16


def _params(semantics):
    return pltpu.CompilerParams(dimension_semantics=semantics,
                                vmem_limit_bytes=V7X_VMEM_LIMIT_BYTES)


def _rope_kernel(cos_ref, sin_ref, *, rows, half):
    r0 = pl.program_id(0) * rows
    pos = (lax.broadcasted_iota(jnp.int32, (rows, half), 0) + r0).astype(F32)
    idx = lax.broadcasted_iota(jnp.int32, (rows, half), 1).astype(F32)
    inv = jnp.exp(-(idx / half) * jnp.log(F32(ROPE_BASE)))
    ang = pos * inv
    cos_ref[...] = jnp.cos(ang)
    sin_ref[...] = jnp.sin(ang)


def _rope_table(seq, half, rows=512):
    return pl.pallas_call(
        functools.partial(_rope_kernel, rows=rows, half=half),
        grid=(seq // rows,),
        out_specs=[pl.BlockSpec((rows, half), lambda i: (i, 0))] * 2,
        out_shape=[jax.ShapeDtypeStruct((seq, half), F32)] * 2,
        compiler_params=_params(("parallel",)),
        name="rope_table",
    )()


def _rmsnorm_kernel(x_ref, g_ref, h_ref):
    x = x_ref[...]
    ms = jnp.mean(x * x, axis=-1, keepdims=True)
    h_ref[...] = (x * lax.rsqrt(ms + RMS_EPS) * g_ref[...]).astype(h_ref.dtype)


def _rmsnorm(x, g, *, tm):
    t, d = x.shape
    return pl.pallas_call(
        _rmsnorm_kernel,
        grid=(t // tm,),
        in_specs=[pl.BlockSpec((tm, d), lambda i: (i, 0)),
                  pl.BlockSpec((1, d), lambda i: (0, 0))],
        out_specs=pl.BlockSpec((tm, d), lambda i: (i, 0)),
        out_shape=jax.ShapeDtypeStruct((t, d), BF16),
        compiler_params=_params(("parallel",)),
        name="rmsnorm",
    )(x, g.reshape(1, d))


def _in_proj_kernel(h_ref, w_ref, cos_ref, sin_ref, qkv_ref, rest_ref, *,
                    n_rot, n_bf16, head_dim):
    j = pl.program_id(1)
    half = head_dim // 2

    def dot():
        return jnp.dot(h_ref[...], w_ref[...], preferred_element_type=F32)

    @pl.when(j < n_rot)
    def _():
        acc = dot()
        cos = cos_ref[...]
        sin = sin_ref[...]
        for hd in range(acc.shape[1] // head_dim):
            lo = slice(hd * head_dim, hd * head_dim + half)
            hi = slice(hd * head_dim + half, (hd + 1) * head_dim)
            t1, t2 = acc[:, lo], acc[:, hi]
            qkv_ref[:, lo] = (t1 * cos - t2 * sin).astype(qkv_ref.dtype)
            qkv_ref[:, hi] = (t2 * cos + t1 * sin).astype(qkv_ref.dtype)

    @pl.when(jnp.logical_and(j >= n_rot, j < n_bf16))
    def _():
        qkv_ref[...] = dot().astype(qkv_ref.dtype)

    @pl.when(j >= n_bf16)
    def _():
        rest_ref[...] = dot()


def _in_proj(h, w, cos, sin, *, tm, tn, rot_width, bf16_width, head_dim):
    t, d = h.shape
    n = w.shape[1]
    seq = cos.shape[0]
    assert rot_width % tn == 0 and bf16_width % tn == 0 and tn % head_dim == 0
    assert seq % tm == 0
    n_rot, n_bf16 = rot_width // tn, bf16_width // tn
    pos_blocks = seq // tm
    return pl.pallas_call(
        functools.partial(_in_proj_kernel, n_rot=n_rot, n_bf16=n_bf16, head_dim=head_dim),
        grid=(t // tm, n // tn),
        in_specs=[
            pl.BlockSpec((tm, d), lambda i, j: (i, 0)),
            pl.BlockSpec((d, tn), lambda i, j: (0, j)),
            pl.BlockSpec((tm, head_dim // 2), lambda i, j: (i % pos_blocks, 0)),
            pl.BlockSpec((tm, head_dim // 2), lambda i, j: (i % pos_blocks, 0)),
        ],
        out_specs=[
            pl.BlockSpec((tm, tn), lambda i, j: (i, jnp.minimum(j, n_bf16 - 1))),
            pl.BlockSpec((tm, tn), lambda i, j: (i, jnp.maximum(j - n_bf16, 0))),
        ],
        out_shape=[jax.ShapeDtypeStruct((t, bf16_width), BF16),
                   jax.ShapeDtypeStruct((t, n - bf16_width), F32)],
        compiler_params=_params(("parallel", "arbitrary")),
        name="in_proj",
    )(h, w, cos, sin)


def _slab_rows(w_rows, steps):
    rows, rem = divmod(w_rows, steps)
    assert rem == 0 and rows % (2 * SUBLANES) == 0, (w_rows, steps)
    return rows


def _retention_kernel(q_ref, k_ref, v_ref, g_ref, gn_ref, wn_ref, o_ref, wnb_ref,
                      state_ref, dec_ref, qd_ref, kd_ref, cd_ref, *, chunk, n_sub, head_dim):
    scale = head_dim ** -0.5
    assert math.frexp(scale)[0] == 0.5

    @pl.when(pl.program_id(2) == 0)
    def _():
        state_ref[...] = jnp.zeros_like(state_ref)
        hf = jnp.full((1, 1), pl.program_id(1), jnp.int32).astype(F32)
        log_g = jnp.log1p(-jnp.exp2(-5.0 - hf))
        row = lax.broadcasted_iota(jnp.int32, (chunk, chunk), 0)
        col = lax.broadcasted_iota(jnp.int32, (chunk, chunk), 1)
        diff = (row - col).astype(F32)
        dec_ref[...] = jnp.where(diff >= 0, jnp.exp(log_g * jnp.maximum(diff, 0.0)), 0.0) * scale
        idx = lax.broadcasted_iota(jnp.int32, (chunk, head_dim), 0).astype(F32)
        qd_ref[...] = jnp.exp(log_g * (idx + 1.0))
        kd_ref[...] = jnp.exp(log_g * (chunk - 1.0 - idx)) * scale
        cd_ref[...] = jnp.broadcast_to(jnp.exp(log_g * chunk), cd_ref.shape)

    wnb_ref[...] = wn_ref[...].astype(BF16)

    state = state_ref[...]
    for c in range(n_sub):
        rows = pl.ds(c * chunk, chunk)
        qb = q_ref[0, rows, :]
        kb = k_ref[0, rows, :]
        vb = v_ref[0, rows, :]
        scores = lax.dot_general(qb, kb, (((1,), (1,)), ((), ())),
                                 preferred_element_type=F32) * dec_ref[...]
        out = (jnp.dot(scores.astype(BF16), vb, preferred_element_type=F32)
               + qd_ref[...] * jnp.dot(qb, state.astype(BF16), preferred_element_type=F32))
        kd = (kb.astype(F32) * kd_ref[...]).astype(BF16)
        kv = lax.dot_general(kd, vb, (((0,), (0,)), ((), ())), preferred_element_type=F32)
        state = cd_ref[...] * state + kv

        mu = jnp.mean(out, axis=-1, keepdims=True)
        cen = out - mu
        var = jnp.mean(cen * cen, axis=-1, keepdims=True)
        o = cen * lax.rsqrt(var + GN_EPS) * gn_ref[...]
        o_ref[0, rows, :] = (jax.nn.silu(g_ref[0, rows, :]) * o).astype(o_ref.dtype)
    state_ref[...] = state


def _retention(qkv, rest, gn_g, w_next, *, chunk, rows):
    b, s, _ = qkv.shape
    h, d = RET_HEADS, RET_HEAD_DIM
    nc = s // rows
    wn_rows, wn_cols = w_next.shape
    sr = _slab_rows(wn_rows, b * h * nc)
    slab = pl.BlockSpec((sr, wn_cols), lambda bi, hi, ci: ((bi * h + hi) * nc + ci, 0))

    def col(off):
        return pl.BlockSpec((1, rows, d), lambda bi, hi, ci: (bi, ci, off + hi))

    return pl.pallas_call(
        functools.partial(_retention_kernel, chunk=chunk, n_sub=rows // chunk, head_dim=d),
        grid=(b, h, nc),
        in_specs=[
            col(0), col(h), col(2 * h), col(0),
            pl.BlockSpec((1, d), lambda bi, hi, ci: (0, hi)),
            slab,
        ],
        out_specs=[pl.BlockSpec((1, rows, d), lambda bi, hi, ci: (bi, ci, hi)), slab],
        out_shape=[jax.ShapeDtypeStruct((b, s, h * d), BF16),
                   jax.ShapeDtypeStruct((wn_rows, wn_cols), BF16)],
        scratch_shapes=[
            pltpu.VMEM((d, d), F32),
            pltpu.VMEM((chunk, chunk), F32),
            pltpu.VMEM((chunk, d), F32),
            pltpu.VMEM((chunk, d), F32),
            pltpu.VMEM((1, d), F32),
        ],
        compiler_params=_params(("parallel", "parallel", "arbitrary")),
        name="retention",
    )(qkv, qkv, qkv, rest, gn_g.reshape(1, h * d), w_next)


def _rglru_kernel(xr_ref, yr_ref, cw_ref, cb_ref, wg_ref, ba_ref, bx_ref, lam_ref, ng_ref,
                  wn_ref, o_ref, wnb_ref, ext_ref, a_ref, u_ref, h_ref, *, tc):
    @pl.when(pl.program_id(1) == 0)
    def _():
        ext_ref[0:SUBLANES, :] = jnp.zeros((SUBLANES, ext_ref.shape[1]), F32)
        h_ref[...] = jnp.zeros_like(h_ref)

    wnb_ref[...] = wn_ref[...].astype(BF16)

    ext_ref[SUBLANES:SUBLANES + tc, :] = xr_ref[0]
    xe = ext_ref[...]
    acc = cw_ref[0:1, :] * xe
    for j in range(1, CONV_WIDTH):
        acc = cw_ref[j:j + 1, :] * xe + pltpu.roll(acc, 1, axis=0)
    xc = acc[SUBLANES:, :] + cb_ref[...]
    ext_ref[0:SUBLANES, :] = ext_ref[tc:tc + SUBLANES, :]

    sp = jax.nn.softplus(-lam_ref[...])
    for n in range(LRU_BLOCKS):
        sl = slice(n * LRU_BLOCK_DIM, (n + 1) * LRU_BLOCK_DIM)
        xb = xc[:, sl]
        gates = jnp.dot(xb.astype(BF16), wg_ref[n], preferred_element_type=F32)
        r = jax.nn.sigmoid(gates[:, :LRU_BLOCK_DIM] + ba_ref[:, sl])
        i = jax.nn.sigmoid(gates[:, LRU_BLOCK_DIM:] + bx_ref[:, sl])
        log_a = -LRU_C * r * sp[:, sl]
        a = jnp.exp(log_a)
        a_ref[:, sl] = a
        one_minus_a2 = -jnp.tanh(log_a) * (1.0 + a * a)
        u_ref[:, sl] = jnp.sqrt(one_minus_a2) * (i * xb)

    def step(t, h):
        h = a_ref[pl.ds(t, 1), :] * h + u_ref[pl.ds(t, 1), :]
        u_ref[pl.ds(t, 1), :] = h
        return h

    h_ref[...] = lax.fori_loop(0, tc, step, h_ref[...], unroll=8)

    y = u_ref[...] * jax.nn.gelu(yr_ref[0], approximate=True)
    ms = jnp.mean(y * y, axis=-1, keepdims=True)
    o_ref[0] = (y * lax.rsqrt(ms + RMS_EPS) * ng_ref[...]).astype(o_ref.dtype)


def _rglru(proj, conv_w, conv_b, wg, ba, bx, lam, ng, w_next, *, tc):
    b, s, pw = proj.shape
    w = LRU_BLOCKS * LRU_BLOCK_DIM
    xr_blk = (pw - 2 * w) // w
    nt = s // tc
    wn_rows, wn_cols = w_next.shape
    sr = _slab_rows(wn_rows, b * nt)
    slab = pl.BlockSpec((sr, wn_cols), lambda bi, ti: (bi * nt + ti, 0))
    row = lambda a: a.reshape(1, w)
    vec = pl.BlockSpec((1, w), lambda bi, ti: (0, 0))
    return pl.pallas_call(
        functools.partial(_rglru_kernel, tc=tc),
        grid=(b, nt),
        in_specs=[
            pl.BlockSpec((1, tc, w), lambda bi, ti: (bi, ti, xr_blk)),
            pl.BlockSpec((1, tc, w), lambda bi, ti: (bi, ti, xr_blk + 1)),
            pl.BlockSpec((CONV_WIDTH, w), lambda bi, ti: (0, 0)),
            vec,
            pl.BlockSpec((LRU_BLOCKS, LRU_BLOCK_DIM, 2 * LRU_BLOCK_DIM),
                         lambda bi, ti: (0, 0, 0)),
            vec, vec, vec, vec,
            slab,
        ],
        out_specs=[pl.BlockSpec((1, tc, w), lambda bi, ti: (bi, ti, 0)), slab],
        out_shape=[jax.ShapeDtypeStruct((b, s, w), BF16),
                   jax.ShapeDtypeStruct((wn_rows, wn_cols), BF16)],
        scratch_shapes=[
            pltpu.VMEM((tc + SUBLANES, w), F32),
            pltpu.VMEM((tc, w), F32),
            pltpu.VMEM((tc, w), F32),
            pltpu.VMEM((1, w), F32),
        ],
        compiler_params=_params(("parallel", "arbitrary")),
        name="rglru",
    )(proj, proj, conv_w, row(conv_b), wg, row(ba), row(bx), row(lam), row(ng), w_next)


def _out_proj_kernel(ret_ref, lru_ref, wr_ref, wl_ref, x_ref, g_ref, o_ref, xg_ref, ssq_ref):
    @pl.when(pl.program_id(1) == 0)
    def _():
        ssq_ref[...] = jnp.zeros_like(ssq_ref)

    acc = jnp.dot(ret_ref[...], wr_ref[...], preferred_element_type=F32)
    acc = acc + jnp.dot(lru_ref[...], wl_ref[...], preferred_element_type=F32)
    x1 = x_ref[...] + acc
    o_ref[...] = x1
    xg_ref[...] = (x1 * g_ref[...]).astype(BF16)
    sq = x1 * x1
    part = sq[:, :LANES]
    for c in range(1, sq.shape[1] // LANES):
        part = part + sq[:, c * LANES:(c + 1) * LANES]
    ssq_ref[...] += part


def _out_proj(ret, lru, w_out, x, g, *, tm, tn):
    t, d = x.shape
    kr, kl = ret.shape[1], lru.shape[1]
    return pl.pallas_call(
        _out_proj_kernel,
        grid=(t // tm, d // tn),
        in_specs=[
            pl.BlockSpec((tm, kr), lambda i, j: (i, 0)),
            pl.BlockSpec((tm, kl), lambda i, j: (i, 0)),
            pl.BlockSpec((kr, tn), lambda i, j: (0, j)),
            pl.BlockSpec((kl, tn), lambda i, j: (kr // kl, j)),
            pl.BlockSpec((tm, tn), lambda i, j: (i, j)),
            pl.BlockSpec((1, tn), lambda i, j: (0, j)),
        ],
        out_specs=[
            pl.BlockSpec((tm, tn), lambda i, j: (i, j)),
            pl.BlockSpec((tm, tn), lambda i, j: (i, j)),
            pl.BlockSpec((tm, LANES), lambda i, j: (i, 0)),
        ],
        out_shape=[
            jax.ShapeDtypeStruct((t, d), F32),
            jax.ShapeDtypeStruct((t, d), BF16),
            jax.ShapeDtypeStruct((t, LANES), F32),
        ],
        compiler_params=_params(("parallel", "arbitrary")),
        name="out_proj",
    )(ret, lru, w_out, w_out, x, g.reshape(1, d))


def _up_kernel(xg_ref, ssq_ref, w_ref, wn_ref, o_ref, wnb_ref, *, d_model):
    ms = jnp.sum(ssq_ref[...], axis=-1, keepdims=True) / d_model
    r = lax.rsqrt(ms + RMS_EPS)
    acc = jnp.dot(xg_ref[...], w_ref[...], preferred_element_type=F32)
    o_ref[...] = jnp.square(jnp.maximum(acc * r, 0.0)).astype(o_ref.dtype)
    wnb_ref[...] = wn_ref[...].astype(BF16)


def _up(xg, ssq, w, w_next, *, tm, tn):
    t, d = xg.shape
    n = w.shape[1]
    ni, nj = t // tm, n // tn
    wn_rows, wn_cols = w_next.shape
    sr = _slab_rows(wn_rows, ni * nj)
    slab = pl.BlockSpec((sr, wn_cols), lambda i, j: (i * nj + j, 0))
    return pl.pallas_call(
        functools.partial(_up_kernel, d_model=d),
        grid=(ni, nj),
        in_specs=[
            pl.BlockSpec((tm, d), lambda i, j: (i, 0)),
            pl.BlockSpec((tm, LANES), lambda i, j: (i, 0)),
            pl.BlockSpec((d, tn), lambda i, j: (0, j)),
            slab,
        ],
        out_specs=[pl.BlockSpec((tm, tn), lambda i, j: (i, j)), slab],
        out_shape=[jax.ShapeDtypeStruct((t, n), BF16),
                   jax.ShapeDtypeStruct((wn_rows, wn_cols), BF16)],
        compiler_params=_params(("parallel", "arbitrary")),
        name="mlp_up",
    )(xg, ssq, w, w_next)


def _down_norm_kernel(a_ref, w_ref, x_ref, g_ref, o_ref, r_ref, *, tn):
    kk = pl.program_id(1)
    d = o_ref.shape[1]
    xs = x_ref.shape[1]

    last = pl.num_programs(1) - 1
    lanes = pl.ds(pl.multiple_of(kk * xs, xs), xs)

    def panels(first, final):
        if not first:
            o_ref[:, lanes] += x_ref[...]
        ssq = None
        for c in range(d // tn):
            cols = slice(c * tn, (c + 1) * tn)
            val = jnp.dot(a_ref[...], w_ref[:, cols], preferred_element_type=F32)
            if not first:
                val = o_ref[:, cols] + val
            o_ref[:, cols] = val
            if final:
                sq = val * val
                for l in range(tn // LANES):
                    blk = sq[:, l * LANES:(l + 1) * LANES]
                    ssq = blk if ssq is None else ssq + blk
        if first:
            o_ref[:, lanes] += x_ref[...]
        if final:
            ms = jnp.sum(ssq, axis=-1, keepdims=True) / d
            r_ref[...] = jnp.broadcast_to(lax.rsqrt(ms + RMS_EPS), r_ref.shape)

    pl.when(kk == 0)(functools.partial(panels, True, False))
    pl.when(jnp.logical_and(kk != 0, kk != last))(functools.partial(panels, False, False))
    pl.when(kk == last)(functools.partial(panels, False, True))

    @pl.when(kk == last)
    def _():
        def apply_scale(r, carry):
            rows = pl.ds(pl.multiple_of(r * NORM_ROWS, NORM_ROWS), NORM_ROWS)
            scale = r_ref[rows, :]
            for l in range(d // LANES):
                cols = slice(l * LANES, (l + 1) * LANES)
                o_ref[rows, cols] = o_ref[rows, cols] * scale * g_ref[:, cols]
            return carry

        lax.fori_loop(0, o_ref.shape[0] // NORM_ROWS, apply_scale, 0, unroll=2)


def _down_norm(act, w_down, x, g, *, tm, tk, tn):
    t, d = x.shape
    f = act.shape[1]
    nk = f // tk
    assert nk > 1 and d % nk == 0 and (d // nk) % LANES == 0
    return pl.pallas_call(
        functools.partial(_down_norm_kernel, tn=tn),
        grid=(t // tm, nk),
        in_specs=[
            pl.BlockSpec((tm, tk), lambda i, k: (i, k)),
            pl.BlockSpec((tk, d), lambda i, k: (k, 0)),
            pl.BlockSpec((tm, d // nk), lambda i, k: (i, k)),
            pl.BlockSpec((1, d), lambda i, k: (0, 0)),
        ],
        out_specs=pl.BlockSpec((tm, d), lambda i, k: (i, 0)),
        out_shape=jax.ShapeDtypeStruct((t, d), F32),
        scratch_shapes=[pltpu.VMEM((tm, LANES), F32)],
        compiler_params=_params(("parallel", "arbitrary")),
        name="down_norm",
    )(act, w_down, x, g.reshape(1, d))


def kernel(x, norm1_g, w_in, ret_gn_g, conv_w, conv_b, gate_a_w, gate_a_b, gate_x_w, gate_x_b,
           lru_lambda, lru_norm_g, w_out, norm2_g, w_up, w_down, normf_g):
    b, s, d = x.shape
    assert w_in.shape[0] == 1, "single-layer problem"
    cos, sin = _rope_table(s, RET_HEAD_DIM // 2)
    xt = x.reshape(b * s, d)
    ret_width = RET_HEADS * RET_HEAD_DIM
    h = _rmsnorm(xt, norm1_g[0], tm=512)
    qkv, rest = _in_proj(h, w_in[0].astype(BF16), cos, sin, tm=1024, tn=1024,
                         rot_width=2 * ret_width, bf16_width=3 * ret_width,
                         head_dim=RET_HEAD_DIM)
    qkv = qkv.reshape(b, s, -1)
    rest = rest.reshape(b, s, -1)
    ret, w_out_b = _retention(qkv, rest, ret_gn_g[0], w_out[0], chunk=256, rows=1024)
    wg = jnp.concatenate([gate_a_w[0], gate_x_w[0]], axis=-1).astype(BF16)
    lru, w_up_b = _rglru(rest, conv_w[0], conv_b[0], wg, gate_a_b[0], gate_x_b[0],
                         lru_lambda[0], lru_norm_g[0], w_up[0], tc=256)
    x1, xg, ssq = _out_proj(ret.reshape(b * s, -1), lru.reshape(b * s, -1),
                            w_out_b, xt, norm2_g[0], tm=1024, tn=512)
    act, w_down_b = _up(xg, ssq, w_up_b, w_down[0], tm=1024, tn=1024)
    out = _down_norm(act, w_down_b, x1, normf_g, tm=1024, tk=1024, tn=1024)
    return out.reshape(b, s, d)
```

```python
import functools
import math

import jax
import jax.numpy as jnp
from jax import lax
from jax.experimental import pallas as pl
from jax.experimental.pallas import tpu as pltpu

RET_HEADS = 8
RET_HEAD_DIM = 256
LRU_BLOCKS = 16
LRU_BLOCK_DIM = 128
CONV_WIDTH = 4
ROPE_BASE = 10000.0
LRU_C = 8.0
RMS_EPS = 1e-6
GN_EPS = 1e-5

SUBLANES = 8
LANES = 128
V7X_VMEM_LIMIT_BYTES = 60 * 1024 * 1024
NORM_ROWS = 4 * SUBLANES
SCAN_SEG = 4
SCAN_ROWS = SCAN_SEG * SUBLANES

F32 = jnp.float32
BF16 = jnp.bfloat16


def _params(semantics):
    return pltpu.CompilerParams(dimension_semantics=semantics,
                                vmem_limit_bytes=V7X_VMEM_LIMIT_BYTES)


def _rope_kernel(cos_ref, sin_ref, *, rows, half):
    r0 = pl.program_id(0) * rows
    pos = (lax.broadcasted_iota(jnp.int32, (rows, half), 0) + r0).astype(F32)
    idx = lax.broadcasted_iota(jnp.int32, (rows, half), 1).astype(F32)
    inv = jnp.exp(-(idx / half) * jnp.log(F32(ROPE_BASE)))
    ang = pos * inv
    cos_ref[...] = jnp.cos(ang)
    sin_ref[...] = jnp.sin(ang)


def _rope_table(seq, half, rows=512):
    return pl.pallas_call(
        functools.partial(_rope_kernel, rows=rows, half=half),
        grid=(seq // rows,),
        out_specs=[pl.BlockSpec((rows, half), lambda i: (i, 0))] * 2,
        out_shape=[jax.ShapeDtypeStruct((seq, half), F32)] * 2,
        compiler_params=_params(("parallel",)),
        name="rope_table",
    )()


def _in_proj_kernel(x_ref, g_ref, w_ref, cos_ref, sin_ref, qkv_ref, rest_ref, h_ref, *,
                    n_norm, n_rot, n_bf16, head_dim):
    i = pl.program_id(0)
    j = pl.program_id(1)
    nj = pl.num_programs(1)
    half = head_dim // 2
    chunk = x_ref.shape[0]
    parity = i % 2

    def norm_ahead():
        x = x_ref[...]
        ms = jnp.mean(x * x, axis=-1, keepdims=True)
        rows = pl.ds(pl.multiple_of(j * chunk, chunk), chunk)
        h_ref[parity, rows, :] = (x * lax.rsqrt(ms + RMS_EPS) * g_ref[...]).astype(BF16)

    def tile(kind, with_norm):
        acc = jnp.dot(h_ref[1 - parity], w_ref[...], preferred_element_type=F32)
        if kind == "rotary":
            cos = cos_ref[...]
            sin = sin_ref[...]
            for hd in range(acc.shape[1] // head_dim):
                lo = slice(hd * head_dim, hd * head_dim + half)
                hi = slice(hd * head_dim + half, (hd + 1) * head_dim)
                t1, t2 = acc[:, lo], acc[:, hi]
                qkv_ref[:, lo] = (t1 * cos - t2 * sin).astype(qkv_ref.dtype)
                qkv_ref[:, hi] = (t2 * cos + t1 * sin).astype(qkv_ref.dtype)
        elif kind == "bf16":
            qkv_ref[...] = acc.astype(qkv_ref.dtype)
        else:
            rest_ref[...] = acc
        if with_norm:
            norm_ahead()

    pl.when(jnp.logical_and(i == 0, j < n_norm))(norm_ahead)

    for kind, lo, hi in (("rotary", 0, n_rot), ("bf16", n_rot, n_bf16), ("f32", n_bf16, None)):
        for with_norm, a, b in ((True, lo, n_norm if hi is None else min(hi, n_norm)),
                                (False, max(lo, n_norm), hi)):
            if b is not None and a >= b:
                continue
            cond = jnp.logical_and(i > 0, j >= a)
            cond = jnp.logical_and(cond, j < (nj if b is None else b))
            pl.when(cond)(functools.partial(tile, kind, with_norm))


def _in_proj(x, g, w, cos, sin, *, tm, tn, n_norm, rot_width, bf16_width, head_dim):
    t, d = x.shape
    n = w.shape[1]
    seq = cos.shape[0]
    assert rot_width % tn == 0 and bf16_width % tn == 0 and tn % head_dim == 0
    assert seq % tm == 0
    n_rot, n_bf16 = rot_width // tn, bf16_width // tn
    ni = t // tm
    pos_blocks = seq // tm
    chunk = tm // n_norm
    assert n_norm <= n // tn and tm % n_norm == 0 and chunk % (2 * SUBLANES) == 0

    def mat_block(i):
        return jnp.maximum(i - 1, 0)

    def x_index(i, j):
        return jnp.minimum(i, ni - 1) * n_norm + jnp.minimum(j, n_norm - 1), 0

    def first(i, blk):
        return jnp.where(i == 0, 0, blk)

    return pl.pallas_call(
        functools.partial(_in_proj_kernel, n_norm=n_norm, n_rot=n_rot, n_bf16=n_bf16,
                          head_dim=head_dim),
        grid=(ni + 1, n // tn),
        in_specs=[
            pl.BlockSpec((chunk, d), x_index),
            pl.BlockSpec((1, d), lambda i, j: (0, 0)),
            pl.BlockSpec((d, tn), lambda i, j: (0, first(i, j))),
            pl.BlockSpec((tm, head_dim // 2), lambda i, j: (mat_block(i) % pos_blocks, 0)),
            pl.BlockSpec((tm, head_dim // 2), lambda i, j: (mat_block(i) % pos_blocks, 0)),
        ],
        out_specs=[
            pl.BlockSpec((tm, tn),
                         lambda i, j: (mat_block(i), first(i, jnp.minimum(j, n_bf16 - 1)))),
            pl.BlockSpec((tm, tn),
                         lambda i, j: (mat_block(i), first(i, jnp.maximum(j - n_bf16, 0)))),
        ],
        out_shape=[jax.ShapeDtypeStruct((t, bf16_width), BF16),
                   jax.ShapeDtypeStruct((t, n - bf16_width), F32)],
        scratch_shapes=[pltpu.VMEM((2, tm, d), BF16)],
        compiler_params=_params(("arbitrary", "arbitrary")),
        name="in_proj",
    )(x, g.reshape(1, d), w, cos, sin)


def _slab_rows(w_rows, steps):
    rows, rem = divmod(w_rows, steps)
    assert rem == 0 and rows % (2 * SUBLANES) == 0, (w_rows, steps)
    return rows


def _retention_kernel(q_ref, k_ref, v_ref, g_ref, gn_ref, wn_ref, o_ref, wnb_ref,
                      state_ref, dec_ref, qd_ref, kd_ref, cd_ref, *, chunk, n_sub, head_dim):
    scale = head_dim ** -0.5
    assert math.frexp(scale)[0] == 0.5

    @pl.when(pl.program_id(2) == 0)
    def _():
        state_ref[...] = jnp.zeros_like(state_ref)
        hf = jnp.full((1, 1), pl.program_id(1), jnp.int32).astype(F32)
        log_g = jnp.log1p(-jnp.exp2(-5.0 - hf))
        row = lax.broadcasted_iota(jnp.int32, (chunk, chunk), 0)
        col = lax.broadcasted_iota(jnp.int32, (chunk, chunk), 1)
        diff = (row - col).astype(F32)
        dec_ref[...] = jnp.where(diff >= 0, jnp.exp(log_g * jnp.maximum(diff, 0.0)), 0.0) * scale
        idx = lax.broadcasted_iota(jnp.int32, (chunk, head_dim), 0).astype(F32)
        qd_ref[...] = jnp.exp(log_g * (idx + 1.0))
        kd_ref[...] = jnp.exp(log_g * (chunk - 1.0 - idx)) * scale
        cd_ref[...] = jnp.broadcast_to(jnp.exp(log_g * chunk), cd_ref.shape)

    wnb_ref[...] = wn_ref[...].astype(BF16)

    state = state_ref[...]
    for c in range(n_sub):
        rows = pl.ds(c * chunk, chunk)
        qb = q_ref[0, rows, :]
        kb = k_ref[0, rows, :]
        vb = v_ref[0, rows, :]
        scores = lax.dot_general(qb, kb, (((1,), (1,)), ((), ())),
                                 preferred_element_type=F32) * dec_ref[...]
        out = (jnp.dot(scores.astype(BF16), vb, preferred_element_type=F32)
               + qd_ref[...] * jnp.dot(qb, state.astype(BF16), preferred_element_type=F32))
        kd = (kb.astype(F32) * kd_ref[...]).astype(BF16)
        kv = lax.dot_general(kd, vb, (((0,), (0,)), ((), ())), preferred_element_type=F32)
        state = cd_ref[...] * state + kv

        mu = jnp.mean(out, axis=-1, keepdims=True)
        cen = out - mu
        var = jnp.mean(cen * cen, axis=-1, keepdims=True)
        o = cen * lax.rsqrt(var + GN_EPS) * gn_ref[...]
        o_ref[0, rows, :] = (jax.nn.silu(g_ref[0, rows, :]) * o).astype(o_ref.dtype)
    state_ref[...] = state


def _retention(qkv, rest, gn_g, w_next, *, chunk, rows):
    b, s, _ = qkv.shape
    h, d = RET_HEADS, RET_HEAD_DIM
    nc = s // rows
    wn_rows, wn_cols = w_next.shape
    sr = _slab_rows(wn_rows, b * h * nc)
    slab = pl.BlockSpec((sr, wn_cols), lambda bi, hi, ci: ((bi * h + hi) * nc + ci, 0))

    def col(off):
        return pl.BlockSpec((1, rows, d), lambda bi, hi, ci: (bi, ci, off + hi))

    return pl.pallas_call(
        functools.partial(_retention_kernel, chunk=chunk, n_sub=rows // chunk, head_dim=d),
        grid=(b, h, nc),
        in_specs=[
            col(0), col(h), col(2 * h), col(0),
            pl.BlockSpec((1, d), lambda bi, hi, ci: (0, hi)),
            slab,
        ],
        out_specs=[pl.BlockSpec((1, rows, d), lambda bi, hi, ci: (bi, ci, hi)), slab],
        out_shape=[jax.ShapeDtypeStruct((b, s, h * d), BF16),
                   jax.ShapeDtypeStruct((wn_rows, wn_cols), BF16)],
        scratch_shapes=[
            pltpu.VMEM((d, d), F32),
            pltpu.VMEM((chunk, chunk), F32),
            pltpu.VMEM((chunk, d), F32),
            pltpu.VMEM((chunk, d), F32),
            pltpu.VMEM((1, d), F32),
        ],
        compiler_params=_params(("parallel", "parallel", "arbitrary")),
        name="retention",
    )(qkv, qkv, qkv, rest, gn_g.reshape(1, h * d), w_next)


def _rglru_kernel(xr_ref, yr_ref, cw_ref, cb_ref, wg_ref, ba_ref, bx_ref, lam_ref, ng_ref,
                  wn_ref, o_ref, wnb_ref, ext_ref, a_ref, u_ref, h_ref, *, tc):
    @pl.when(pl.program_id(1) == 0)
    def _():
        ext_ref[0:SUBLANES, :] = jnp.zeros((SUBLANES, ext_ref.shape[1]), F32)
        h_ref[...] = jnp.zeros_like(h_ref)

    wnb_ref[...] = wn_ref[...].astype(BF16)

    ext_ref[SUBLANES:SUBLANES + tc, :] = xr_ref[0]
    xe = ext_ref[...]
    acc = cw_ref[0:1, :] * xe
    for j in range(1, CONV_WIDTH):
        acc = cw_ref[j:j + 1, :] * xe + pltpu.roll(acc, 1, axis=0)
    xc = acc[SUBLANES:, :] + cb_ref[...]
    ext_ref[0:SUBLANES, :] = ext_ref[tc:tc + SUBLANES, :]

    sp = jax.nn.softplus(-lam_ref[...])
    for n in range(LRU_BLOCKS):
        sl = slice(n * LRU_BLOCK_DIM, (n + 1) * LRU_BLOCK_DIM)
        xb = xc[:, sl]
        gates = jnp.dot(xb.astype(BF16), wg_ref[n], preferred_element_type=F32)
        r = jax.nn.sigmoid(gates[:, :LRU_BLOCK_DIM] + ba_ref[:, sl])
        i = jax.nn.sigmoid(gates[:, LRU_BLOCK_DIM:] + bx_ref[:, sl])
        log_a = -LRU_C * r * sp[:, sl]
        a = jnp.exp(log_a)
        a_ref[n] = a
        one_minus_a2 = -jnp.tanh(log_a) * (1.0 + a * a)
        u_ref[n] = jnp.sqrt(one_minus_a2) * (i * xb)

    sub = lax.broadcasted_iota(jnp.int32, (SUBLANES, LRU_BLOCK_DIM), 0)
    for grp in range(tc // SCAN_ROWS):
        base = grp * SCAN_ROWS
        for n in range(LRU_BLOCKS):
            def seg_rows(jj):
                return pl.ds(base + jj, SUBLANES, stride=SCAN_SEG)

            hs = [u_ref[n, seg_rows(0), :]]
            ps = [a_ref[n, seg_rows(0), :]]
            for jj in range(1, SCAN_SEG):
                a_j = a_ref[n, seg_rows(jj), :]
                hs.append(a_j * hs[-1] + u_ref[n, seg_rows(jj), :])
                ps.append(a_j * ps[-1])
            p_end, h_end = ps[-1], hs[-1]
            shift = 1
            while shift < SUBLANES:
                keep = sub >= shift
                h_prev = pltpu.roll(h_end, shift, axis=0)
                p_prev = pltpu.roll(p_end, shift, axis=0)
                h_end = jnp.where(keep, p_end * h_prev + h_end, h_end)
                p_end = jnp.where(keep, p_end * p_prev, p_end)
                shift *= 2
            h0 = jnp.broadcast_to(h_ref[n, SUBLANES - 1:SUBLANES, :], h_end.shape)
            end_state = h_end + p_end * h0
            h_ref[n] = end_state
            start = jnp.where(sub >= 1, pltpu.roll(end_state, 1, axis=0), h0)
            for jj in range(SCAN_SEG):
                u_ref[n, seg_rows(jj), :] = hs[jj] + ps[jj] * start

    ssq = None
    for n in range(LRU_BLOCKS):
        sl = slice(n * LRU_BLOCK_DIM, (n + 1) * LRU_BLOCK_DIM)
        y = u_ref[n] * jax.nn.gelu(yr_ref[0, :, sl], approximate=True)
        u_ref[n] = y
        ssq = y * y if ssq is None else ssq + y * y
    ms = jnp.sum(ssq, axis=-1, keepdims=True) / (LRU_BLOCKS * LRU_BLOCK_DIM)
    scale = jnp.broadcast_to(lax.rsqrt(ms + RMS_EPS), ssq.shape)
    for n in range(LRU_BLOCKS):
        sl = slice(n * LRU_BLOCK_DIM, (n + 1) * LRU_BLOCK_DIM)
        o_ref[0, :, sl] = (u_ref[n] * scale * ng_ref[:, sl]).astype(o_ref.dtype)


def _rglru(proj, conv_w, conv_b, wg, ba, bx, lam, ng, w_next, *, tc):
    b, s, pw = proj.shape
    w = LRU_BLOCKS * LRU_BLOCK_DIM
    xr_blk = (pw - 2 * w) // w
    nt = s // tc
    wn_rows, wn_cols = w_next.shape
    sr = _slab_rows(wn_rows, b * nt)
    slab = pl.BlockSpec((sr, wn_cols), lambda bi, ti: (bi * nt + ti, 0))
    row = lambda a: a.reshape(1, w)
    vec = pl.BlockSpec((1, w), lambda bi, ti: (0, 0))
    return pl.pallas_call(
        functools.partial(_rglru_kernel, tc=tc),
        grid=(b, nt),
        in_specs=[
            pl.BlockSpec((1, tc, w), lambda bi, ti: (bi, ti, xr_blk)),
            pl.BlockSpec((1, tc, w), lambda bi, ti: (bi, ti, xr_blk + 1)),
            pl.BlockSpec((CONV_WIDTH, w), lambda bi, ti: (0, 0)),
            vec,
            pl.BlockSpec((LRU_BLOCKS, LRU_BLOCK_DIM, 2 * LRU_BLOCK_DIM),
                         lambda bi, ti: (0, 0, 0)),
            vec, vec, vec, vec,
            slab,
        ],
        out_specs=[pl.BlockSpec((1, tc, w), lambda bi, ti: (bi, ti, 0)), slab],
        out_shape=[jax.ShapeDtypeStruct((b, s, w), BF16),
                   jax.ShapeDtypeStruct((wn_rows, wn_cols), BF16)],
        scratch_shapes=[
            pltpu.VMEM((tc + SUBLANES, w), F32),
            pltpu.VMEM((LRU_BLOCKS, tc, LRU_BLOCK_DIM), F32),
            pltpu.VMEM((LRU_BLOCKS, tc, LRU_BLOCK_DIM), F32),
            pltpu.VMEM((LRU_BLOCKS, SUBLANES, LRU_BLOCK_DIM), F32),
        ],
        compiler_params=_params(("parallel", "arbitrary")),
        name="rglru",
    )(proj, proj, conv_w, row(conv_b), wg, row(ba), row(bx), row(lam), row(ng), w_next)


def _out_proj_kernel(ret_ref, lru_ref, wr_ref, wl_ref, x_ref, g_ref, o_ref, xg_ref, ssq_ref):
    @pl.when(pl.program_id(1) == 0)
    def _():
        ssq_ref[...] = jnp.zeros_like(ssq_ref)

    acc = jnp.dot(ret_ref[...], wr_ref[...], preferred_element_type=F32)
    acc = acc + jnp.dot(lru_ref[...], wl_ref[...], preferred_element_type=F32)
    x1 = x_ref[...] + acc
    o_ref[...] = x1
    xg_ref[...] = (x1 * g_ref[...]).astype(BF16)
    sq = x1 * x1
    part = sq[:, :LANES]
    for c in range(1, sq.shape[1] // LANES):
        part = part + sq[:, c * LANES:(c + 1) * LANES]
    ssq_ref[...] += part


def _out_proj(ret, lru, w_out, x, g, *, tm, tn):
    t, d = x.shape
    kr, kl = ret.shape[1], lru.shape[1]
    return pl.pallas_call(
        _out_proj_kernel,
        grid=(t // tm, d // tn),
        in_specs=[
            pl.BlockSpec((tm, kr), lambda i, j: (i, 0)),
            pl.BlockSpec((tm, kl), lambda i, j: (i, 0)),
            pl.BlockSpec((kr, tn), lambda i, j: (0, j)),
            pl.BlockSpec((kl, tn), lambda i, j: (kr // kl, j)),
            pl.BlockSpec((tm, tn), lambda i, j: (i, j)),
            pl.BlockSpec((1, tn), lambda i, j: (0, j)),
        ],
        out_specs=[
            pl.BlockSpec((tm, tn), lambda i, j: (i, j)),
            pl.BlockSpec((tm, tn), lambda i, j: (i, j)),
            pl.BlockSpec((tm, LANES), lambda i, j: (i, 0)),
        ],
        out_shape=[
            jax.ShapeDtypeStruct((t, d), F32),
            jax.ShapeDtypeStruct((t, d), BF16),
            jax.ShapeDtypeStruct((t, LANES), F32),
        ],
        compiler_params=_params(("parallel", "arbitrary")),
        name="out_proj",
    )(ret, lru, w_out, w_out, x, g.reshape(1, d))


def _up_kernel(xg_ref, ssq_ref, w_ref, wn_ref, o_ref, wnb_ref, *, d_model):
    ms = jnp.sum(ssq_ref[...], axis=-1, keepdims=True) / d_model
    r = lax.rsqrt(ms + RMS_EPS)
    acc = jnp.dot(xg_ref[...], w_ref[...], preferred_element_type=F32)
    o_ref[...] = jnp.square(jnp.maximum(acc * r, 0.0)).astype(o_ref.dtype)
    wnb_ref[...] = wn_ref[...].astype(BF16)


def _up(xg, ssq, w, w_next, *, tm, tn):
    t, d = xg.shape
    n = w.shape[1]
    ni, nj = t // tm, n // tn
    wn_rows, wn_cols = w_next.shape
    sr = _slab_rows(wn_rows, ni * nj)
    slab = pl.BlockSpec((sr, wn_cols), lambda i, j: (i * nj + j, 0))
    return pl.pallas_call(
        functools.partial(_up_kernel, d_model=d),
        grid=(ni, nj),
        in_specs=[
            pl.BlockSpec((tm, d), lambda i, j: (i, 0)),
            pl.BlockSpec((tm, LANES), lambda i, j: (i, 0)),
            pl.BlockSpec((d, tn), lambda i, j: (0, j)),
            slab,
        ],
        out_specs=[pl.BlockSpec((tm, tn), lambda i, j: (i, j)), slab],
        out_shape=[jax.ShapeDtypeStruct((t, n), BF16),
                   jax.ShapeDtypeStruct((wn_rows, wn_cols), BF16)],
        compiler_params=_params(("parallel", "arbitrary")),
        name="mlp_up",
    )(xg, ssq, w, w_next)


def _down_norm_kernel(a_ref, w_ref, x_ref, g_ref, o_ref, r_ref, *, tn):
    kk = pl.program_id(1)
    d = o_ref.shape[1]
    xs = x_ref.shape[1]

    last = pl.num_programs(1) - 1
    lanes = pl.ds(pl.multiple_of(kk * xs, xs), xs)

    def panels(first, final):
        if not first:
            o_ref[:, lanes] += x_ref[...]
        ssq = None
        for c in range(d // tn):
            cols = slice(c * tn, (c + 1) * tn)
            val = jnp.dot(a_ref[...], w_ref[:, cols], preferred_element_type=F32)
            if not first:
                val = o_ref[:, cols] + val
            o_ref[:, cols] = val
            if final:
                sq = val * val
                for l in range(tn // LANES):
                    blk = sq[:, l * LANES:(l + 1) * LANES]
                    ssq = blk if ssq is None else ssq + blk
        if first:
            o_ref[:, lanes] += x_ref[...]
        if final:
            ms = jnp.sum(ssq, axis=-1, keepdims=True) / d
            r_ref[...] = jnp.broadcast_to(lax.rsqrt(ms + RMS_EPS), r_ref.shape)

    pl.when(kk == 0)(functools.partial(panels, True, False))
    pl.when(jnp.logical_and(kk != 0, kk != last))(functools.partial(panels, False, False))
    pl.when(kk == last)(functools.partial(panels, False, True))

    @pl.when(kk == last)
    def _():
        def apply_scale(r, carry):
            rows = pl.ds(pl.multiple_of(r * NORM_ROWS, NORM_ROWS), NORM_ROWS)
            scale = r_ref[rows, :]
            for l in range(d // LANES):
                cols = slice(l * LANES, (l + 1) * LANES)
                o_ref[rows, cols] = o_ref[rows, cols] * scale * g_ref[:, cols]
            return carry

        lax.fori_loop(0, o_ref.shape[0] // NORM_ROWS, apply_scale, 0, unroll=2)


def _down_norm(act, w_down, x, g, *, tm, tk, tn):
    t, d = x.shape
    f = act.shape[1]
    nk = f // tk
    assert nk > 1 and d % nk == 0 and (d // nk) % LANES == 0
    return pl.pallas_call(
        functools.partial(_down_norm_kernel, tn=tn),
        grid=(t // tm, nk),
        in_specs=[
            pl.BlockSpec((tm, tk), lambda i, k: (i, k)),
            pl.BlockSpec((tk, d), lambda i, k: (k, 0)),
            pl.BlockSpec((tm, d // nk), lambda i, k: (i, k)),
            pl.BlockSpec((1, d), lambda i, k: (0, 0)),
        ],
        out_specs=pl.BlockSpec((tm, d), lambda i, k: (i, 0)),
        out_shape=jax.ShapeDtypeStruct((t, d), F32),
        scratch_shapes=[pltpu.VMEM((tm, LANES), F32)],
        compiler_params=_params(("parallel", "arbitrary")),
        name="down_norm",
    )(act, w_down, x, g.reshape(1, d))


def kernel(x, norm1_g, w_in, ret_gn_g, conv_w, conv_b, gate_a_w, gate_a_b, gate_x_w, gate_x_b,
           lru_lambda, lru_norm_g, w_out, norm2_g, w_up, w_down, normf_g):
    b, s, d = x.shape
    assert w_in.shape[0] == 1, "single-layer problem"
    cos, sin = _rope_table(s, RET_HEAD_DIM // 2)
    xt = x.reshape(b * s, d)
    ret_width = RET_HEADS * RET_HEAD_DIM
    qkv, rest = _in_proj(xt, norm1_g[0], w_in[0].astype(BF16), cos, sin, tm=1024, tn=1024,
                         n_norm=8,
                         rot_width=2 * ret_width, bf16_width=3 * ret_width,
                         head_dim=RET_HEAD_DIM)
    qkv = qkv.reshape(b, s, -1)
    rest = rest.reshape(b, s, -1)
    ret, w_out_b = _retention(qkv, rest, ret_gn_g[0], w_out[0], chunk=256, rows=1024)
    wg = jnp.concatenate([gate_a_w[0], gate_x_w[0]], axis=-1).astype(BF16)
    lru, w_up_b = _rglru(rest, conv_w[0], conv_b[0], wg, gate_a_b[0], gate_x_b[0],
                         lru_lambda[0], lru_norm_g[0], w_up[0], tc=256)
    x1, xg, ssq = _out_proj(ret.reshape(b * s, -1), lru.reshape(b * s, -1),
                            w_out_b, xt, norm2_g[0], tm=1024, tn=512)
    act, w_down_b = _up(xg, ssq, w_up_b, w_down[0], tm=1024, tn=1024)
    out = _down_norm(act, w_down_b, x1, normf_g, tm=1024, tk=1024, tn=1024)
    return out.reshape(b, s, d)
```

```python
import functools
import math

import jax
import jax.numpy as jnp
from jax import lax
from jax.experimental import pallas as pl
from jax.experimental.pallas import tpu as pltpu

RET_HEADS = 8
RET_HEAD_DIM = 256
LRU_BLOCKS = 16
LRU_BLOCK_DIM = 128
CONV_WIDTH = 4
ROPE_BASE = 10000.0
LRU_C = 8.0
RMS_EPS = 1e-6
GN_EPS = 1e-5

SUBLANES = 8
LANES = 128
V7X_VMEM_LIMIT_BYTES = 60 * 1024 * 1024
NORM_ROWS = 4 * SUBLANES
SCAN_SEG = 4
SCAN_ROWS = SCAN_SEG * SUBLANES

F32 = jnp.float32
BF16 = jnp.bfloat16


def _params(semantics):
    return pltpu.CompilerParams(dimension_semantics=semantics,
                                vmem_limit_bytes=V7X_VMEM_LIMIT_BYTES)


def _rope_kernel(cos_ref, sin_ref, *, rows, half):
    r0 = pl.program_id(0) * rows
    pos = (lax.broadcasted_iota(jnp.int32, (rows, half), 0) + r0).astype(F32)
    idx = lax.broadcasted_iota(jnp.int32, (rows, half), 1).astype(F32)
    inv = jnp.exp(-(idx / half) * jnp.log(F32(ROPE_BASE)))
    ang = pos * inv
    cos_ref[...] = jnp.cos(ang)
    sin_ref[...] = jnp.sin(ang)


def _rope_table(seq, half, rows=512):
    return pl.pallas_call(
        functools.partial(_rope_kernel, rows=rows, half=half),
        grid=(seq // rows,),
        out_specs=[pl.BlockSpec((rows, half), lambda i: (i, 0))] * 2,
        out_shape=[jax.ShapeDtypeStruct((seq, half), F32)] * 2,
        compiler_params=_params(("parallel",)),
        name="rope_table",
    )()


def _in_proj_kernel(x_ref, g_ref, w_ref, cos_ref, sin_ref, qkv_ref, rest_ref, h_ref, *,
                    n_norm, bounds, head_dim):
    i = pl.program_id(0)
    j = pl.program_id(1)
    half = head_dim // 2
    chunk = x_ref.shape[0]
    parity = i % 2

    def norm_ahead():
        x = x_ref[...]
        ms = jnp.mean(x * x, axis=-1, keepdims=True)
        rows = pl.ds(pl.multiple_of(j * chunk, chunk), chunk)
        h_ref[parity, rows, :] = (x * lax.rsqrt(ms + RMS_EPS) * g_ref[...]).astype(BF16)

    def tile(kind, with_norm):
        acc = jnp.dot(h_ref[1 - parity], w_ref[...], preferred_element_type=F32)
        if kind == "rotary":
            cos = cos_ref[...]
            sin = sin_ref[...]
            for hd in range(acc.shape[1] // head_dim):
                lo = slice(hd * head_dim, hd * head_dim + half)
                hi = slice(hd * head_dim + half, (hd + 1) * head_dim)
                t1, t2 = acc[:, lo], acc[:, hi]
                qkv_ref[:, lo] = (t1 * cos - t2 * sin).astype(qkv_ref.dtype)
                qkv_ref[:, hi] = (t2 * cos + t1 * sin).astype(qkv_ref.dtype)
        elif kind == "bf16":
            qkv_ref[...] = acc.astype(qkv_ref.dtype)
        elif kind == "gelu":
            rest_ref[...] = jax.nn.gelu(acc, approximate=True)
        else:
            rest_ref[...] = acc
        if with_norm:
            norm_ahead()

    pl.when(jnp.logical_and(i == 0, j < n_norm))(norm_ahead)

    kinds = ("rotary", "bf16", "f32", "gelu")
    for kind, lo, hi in zip(kinds, bounds[:-1], bounds[1:]):
        for with_norm, a, b in ((True, lo, min(hi, n_norm)), (False, max(lo, n_norm), hi)):
            if a < b:
                cond = jnp.logical_and(i > 0, jnp.logical_and(j >= a, j < b))
                pl.when(cond)(functools.partial(tile, kind, with_norm))


def _in_proj(x, g, w, cos, sin, *, tm, tn, n_norm, ret_width, lru_width, head_dim):
    t, d = x.shape
    n = w.shape[1]
    seq = cos.shape[0]
    assert n == 4 * ret_width + 2 * lru_width
    assert ret_width % tn == 0 and lru_width % tn == 0 and tn % head_dim == 0
    assert seq % tm == 0
    rw, lw = ret_width // tn, lru_width // tn
    bounds = (0, 2 * rw, 3 * rw, 4 * rw + lw, 4 * rw + 2 * lw)
    n_bf16 = 3 * rw
    bf16_width = 3 * ret_width
    ni = t // tm
    pos_blocks = seq // tm
    chunk = tm // n_norm
    assert n_norm <= n // tn and tm % n_norm == 0 and chunk % (2 * SUBLANES) == 0

    def mat_block(i):
        return jnp.maximum(i - 1, 0)

    def x_index(i, j):
        return jnp.minimum(i, ni - 1) * n_norm + jnp.minimum(j, n_norm - 1), 0

    def first(i, blk):
        return jnp.where(i == 0, 0, blk)

    return pl.pallas_call(
        functools.partial(_in_proj_kernel, n_norm=n_norm, bounds=bounds, head_dim=head_dim),
        grid=(ni + 1, n // tn),
        in_specs=[
            pl.BlockSpec((chunk, d), x_index),
            pl.BlockSpec((1, d), lambda i, j: (0, 0)),
            pl.BlockSpec((d, tn), lambda i, j: (0, first(i, j))),
            pl.BlockSpec((tm, head_dim // 2), lambda i, j: (mat_block(i) % pos_blocks, 0)),
            pl.BlockSpec((tm, head_dim // 2), lambda i, j: (mat_block(i) % pos_blocks, 0)),
        ],
        out_specs=[
            pl.BlockSpec((tm, tn),
                         lambda i, j: (mat_block(i), first(i, jnp.minimum(j, n_bf16 - 1)))),
            pl.BlockSpec((tm, tn),
                         lambda i, j: (mat_block(i), first(i, jnp.maximum(j - n_bf16, 0)))),
        ],
        out_shape=[jax.ShapeDtypeStruct((t, bf16_width), BF16),
                   jax.ShapeDtypeStruct((t, n - bf16_width), F32)],
        scratch_shapes=[pltpu.VMEM((2, tm, d), BF16)],
        compiler_params=_params(("arbitrary", "arbitrary")),
        name="in_proj",
    )(x, g.reshape(1, d), w, cos, sin)


def _slab_rows(w_rows, steps):
    rows, rem = divmod(w_rows, steps)
    assert rem == 0 and rows % (2 * SUBLANES) == 0, (w_rows, steps)
    return rows


def _retention_kernel(q_ref, k_ref, v_ref, g_ref, gn_ref, wn_ref, o_ref, wnb_ref,
                      state_ref, dec_ref, qd_ref, kd_ref, cd_ref, *, chunk, n_sub, head_dim):
    scale = head_dim ** -0.5
    assert math.frexp(scale)[0] == 0.5
    group = state_ref.shape[0]

    @pl.when(pl.program_id(2) == 0)
    def _():
        state_ref[...] = jnp.zeros_like(state_ref)
        row = lax.broadcasted_iota(jnp.int32, (chunk, chunk), 0)
        col = lax.broadcasted_iota(jnp.int32, (chunk, chunk), 1)
        diff = (row - col).astype(F32)
        idx = lax.broadcasted_iota(jnp.int32, (chunk, head_dim), 0).astype(F32)
        for hh in range(group):
            head = pl.program_id(1) * group + hh
            hf = jnp.full((1, 1), head, jnp.int32).astype(F32)
            log_g = jnp.log1p(-jnp.exp2(-5.0 - hf))
            dec_ref[hh] = jnp.where(diff >= 0, jnp.exp(log_g * jnp.maximum(diff, 0.0)),
                                    0.0) * scale
            qd_ref[hh] = jnp.exp(log_g * (idx + 1.0))
            kd_ref[hh] = jnp.exp(log_g * (chunk - 1.0 - idx)) * scale
            cd_ref[hh] = jnp.broadcast_to(jnp.exp(log_g * chunk), cd_ref.shape[1:])

    wnb_ref[...] = wn_ref[...].astype(BF16)

    states = [state_ref[hh] for hh in range(group)]
    for c in range(n_sub):
        rows = pl.ds(c * chunk, chunk)
        for hh in range(group):
            cols = slice(hh * head_dim, (hh + 1) * head_dim)
            qb = q_ref[0, rows, cols]
            kb = k_ref[0, rows, cols]
            vb = v_ref[0, rows, cols]
            scores = lax.dot_general(qb, kb, (((1,), (1,)), ((), ())),
                                     preferred_element_type=F32) * dec_ref[hh]
            out = (jnp.dot(scores.astype(BF16), vb, preferred_element_type=F32)
                   + qd_ref[hh] * jnp.dot(qb, states[hh].astype(BF16),
                                          preferred_element_type=F32))
            kd = (kb.astype(F32) * kd_ref[hh]).astype(BF16)
            kv = lax.dot_general(kd, vb, (((0,), (0,)), ((), ())), preferred_element_type=F32)
            states[hh] = cd_ref[hh] * states[hh] + kv

            mu = jnp.mean(out, axis=-1, keepdims=True)
            cen = out - mu
            var = jnp.mean(cen * cen, axis=-1, keepdims=True)
            o = cen * lax.rsqrt(var + GN_EPS) * gn_ref[:, cols]
            o_ref[0, rows, cols] = (jax.nn.silu(g_ref[0, rows, cols]) * o).astype(o_ref.dtype)
    for hh in range(group):
        state_ref[hh] = states[hh]


def _retention(qkv, rest, gn_g, w_next, *, chunk, rows, group):
    b, s, _ = qkv.shape
    h, d = RET_HEADS, RET_HEAD_DIM
    assert h % group == 0
    hg, gd = h // group, group * d
    nc = s // rows
    wn_rows, wn_cols = w_next.shape
    sr = _slab_rows(wn_rows, b * hg * nc)
    slab = pl.BlockSpec((sr, wn_cols), lambda bi, hi, ci: ((bi * hg + hi) * nc + ci, 0))

    def col(off):
        return pl.BlockSpec((1, rows, gd), lambda bi, hi, ci: (bi, ci, off + hi))

    return pl.pallas_call(
        functools.partial(_retention_kernel, chunk=chunk, n_sub=rows // chunk, head_dim=d),
        grid=(b, hg, nc),
        in_specs=[
            col(0), col(hg), col(2 * hg), col(0),
            pl.BlockSpec((1, gd), lambda bi, hi, ci: (0, hi)),
            slab,
        ],
        out_specs=[pl.BlockSpec((1, rows, gd), lambda bi, hi, ci: (bi, ci, hi)), slab],
        out_shape=[jax.ShapeDtypeStruct((b, s, h * d), BF16),
                   jax.ShapeDtypeStruct((wn_rows, wn_cols), BF16)],
        scratch_shapes=[
            pltpu.VMEM((group, d, d), F32),
            pltpu.VMEM((group, chunk, chunk), F32),
            pltpu.VMEM((group, chunk, d), F32),
            pltpu.VMEM((group, chunk, d), F32),
            pltpu.VMEM((group, 1, d), F32),
        ],
        compiler_params=_params(("parallel", "parallel", "arbitrary")),
        name="retention",
    )(qkv, qkv, qkv, rest, gn_g.reshape(1, h * d), w_next)


def _rglru_kernel(xr_ref, yr_ref, cw_ref, cb_ref, wg_ref, ba_ref, bx_ref, lam_ref, ng_ref,
                  wn_ref, o_ref, wnb_ref, ext_ref, a_ref, u_ref, h_ref, *, tc):
    @pl.when(pl.program_id(1) == 0)
    def _():
        ext_ref[0:SUBLANES, :] = jnp.zeros((SUBLANES, ext_ref.shape[1]), F32)
        h_ref[...] = jnp.zeros_like(h_ref)

    wnb_ref[...] = wn_ref[...].astype(BF16)

    ext_ref[SUBLANES:SUBLANES + tc, :] = xr_ref[0]
    xe = ext_ref[...]
    acc = cw_ref[0:1, :] * xe
    for j in range(1, CONV_WIDTH):
        acc = cw_ref[j:j + 1, :] * xe + pltpu.roll(acc, 1, axis=0)
    xc = acc[SUBLANES:, :] + cb_ref[...]
    ext_ref[0:SUBLANES, :] = ext_ref[tc:tc + SUBLANES, :]

    sp = jax.nn.softplus(-lam_ref[...])
    for n in range(LRU_BLOCKS):
        sl = slice(n * LRU_BLOCK_DIM, (n + 1) * LRU_BLOCK_DIM)
        xb = xc[:, sl]
        gates = jnp.dot(xb.astype(BF16), wg_ref[n], preferred_element_type=F32)
        r = jax.nn.sigmoid(gates[:, :LRU_BLOCK_DIM] + ba_ref[:, sl])
        i = jax.nn.sigmoid(gates[:, LRU_BLOCK_DIM:] + bx_ref[:, sl])
        log_a = -LRU_C * r * sp[:, sl]
        a = jnp.exp(log_a)
        a_ref[n] = a
        one_minus_a2 = -jnp.tanh(log_a) * (1.0 + a * a)
        u_ref[n] = jnp.sqrt(one_minus_a2) * (i * xb)

    sub = lax.broadcasted_iota(jnp.int32, (SUBLANES, LRU_BLOCK_DIM), 0)
    for grp in range(tc // SCAN_ROWS):
        base = grp * SCAN_ROWS
        for n in range(LRU_BLOCKS):
            def seg_rows(jj):
                return pl.ds(base + jj, SUBLANES, stride=SCAN_SEG)

            hs = [u_ref[n, seg_rows(0), :]]
            ps = [a_ref[n, seg_rows(0), :]]
            for jj in range(1, SCAN_SEG):
                a_j = a_ref[n, seg_rows(jj), :]
                hs.append(a_j * hs[-1] + u_ref[n, seg_rows(jj), :])
                ps.append(a_j * ps[-1])
            p_end, h_end = ps[-1], hs[-1]
            shift = 1
            while shift < SUBLANES:
                keep = sub >= shift
                h_prev = pltpu.roll(h_end, shift, axis=0)
                p_prev = pltpu.roll(p_end, shift, axis=0)
                h_end = jnp.where(keep, p_end * h_prev + h_end, h_end)
                p_end = jnp.where(keep, p_end * p_prev, p_end)
                shift *= 2
            h0 = jnp.broadcast_to(h_ref[n, SUBLANES - 1:SUBLANES, :], h_end.shape)
            end_state = h_end + p_end * h0
            h_ref[n] = end_state
            start = jnp.where(sub >= 1, pltpu.roll(end_state, 1, axis=0), h0)
            for jj in range(SCAN_SEG):
                u_ref[n, seg_rows(jj), :] = hs[jj] + ps[jj] * start

    ssq = None
    for n in range(LRU_BLOCKS):
        sl = slice(n * LRU_BLOCK_DIM, (n + 1) * LRU_BLOCK_DIM)
        y = u_ref[n] * yr_ref[0, :, sl]
        u_ref[n] = y
        ssq = y * y if ssq is None else ssq + y * y
    ms = jnp.sum(ssq, axis=-1, keepdims=True) / (LRU_BLOCKS * LRU_BLOCK_DIM)
    scale = jnp.broadcast_to(lax.rsqrt(ms + RMS_EPS), ssq.shape)
    for n in range(LRU_BLOCKS):
        sl = slice(n * LRU_BLOCK_DIM, (n + 1) * LRU_BLOCK_DIM)
        o_ref[0, :, sl] = (u_ref[n] * scale * ng_ref[:, sl]).astype(o_ref.dtype)


def _rglru(proj, conv_w, conv_b, wg, ba, bx, lam, ng, w_next, *, tc):
    b, s, pw = proj.shape
    w = LRU_BLOCKS * LRU_BLOCK_DIM
    xr_blk = (pw - 2 * w) // w
    nt = s // tc
    wn_rows, wn_cols = w_next.shape
    sr = _slab_rows(wn_rows, b * nt)
    slab = pl.BlockSpec((sr, wn_cols), lambda bi, ti: (bi * nt + ti, 0))
    row = lambda a: a.reshape(1, w)
    vec = pl.BlockSpec((1, w), lambda bi, ti: (0, 0))
    return pl.pallas_call(
        functools.partial(_rglru_kernel, tc=tc),
        grid=(b, nt),
        in_specs=[
            pl.BlockSpec((1, tc, w), lambda bi, ti: (bi, ti, xr_blk)),
            pl.BlockSpec((1, tc, w), lambda bi, ti: (bi, ti, xr_blk + 1)),
            pl.BlockSpec((CONV_WIDTH, w), lambda bi, ti: (0, 0)),
            vec,
            pl.BlockSpec((LRU_BLOCKS, LRU_BLOCK_DIM, 2 * LRU_BLOCK_DIM),
                         lambda bi, ti: (0, 0, 0)),
            vec, vec, vec, vec,
            slab,
        ],
        out_specs=[pl.BlockSpec((1, tc, w), lambda bi, ti: (bi, ti, 0)), slab],
        out_shape=[jax.ShapeDtypeStruct((b, s, w), BF16),
                   jax.ShapeDtypeStruct((wn_rows, wn_cols), BF16)],
        scratch_shapes=[
            pltpu.VMEM((tc + SUBLANES, w), F32),
            pltpu.VMEM((LRU_BLOCKS, tc, LRU_BLOCK_DIM), F32),
            pltpu.VMEM((LRU_BLOCKS, tc, LRU_BLOCK_DIM), F32),
            pltpu.VMEM((LRU_BLOCKS, SUBLANES, LRU_BLOCK_DIM), F32),
        ],
        compiler_params=_params(("parallel", "arbitrary")),
        name="rglru",
    )(proj, proj, conv_w, row(conv_b), wg, row(ba), row(bx), row(lam), row(ng), w_next)


def _out_proj_kernel(ret_ref, lru_ref, wr_ref, wl_ref, x_ref, g_ref, o_ref, xg_ref, ssq_ref):
    @pl.when(pl.program_id(1) == 0)
    def _():
        ssq_ref[...] = jnp.zeros_like(ssq_ref)

    acc = jnp.dot(ret_ref[...], wr_ref[...], preferred_element_type=F32)
    acc = acc + jnp.dot(lru_ref[...], wl_ref[...], preferred_element_type=F32)
    x1 = x_ref[...] + acc
    o_ref[...] = x1
    xg_ref[...] = (x1 * g_ref[...]).astype(BF16)
    sq = x1 * x1
    part = sq[:, :LANES]
    for c in range(1, sq.shape[1] // LANES):
        part = part + sq[:, c * LANES:(c + 1) * LANES]
    ssq_ref[...] += part


def _out_proj(ret, lru, w_out, x, g, *, tm, tn):
    t, d = x.shape
    kr, kl = ret.shape[1], lru.shape[1]
    return pl.pallas_call(
        _out_proj_kernel,
        grid=(t // tm, d // tn),
        in_specs=[
            pl.BlockSpec((tm, kr), lambda i, j: (i, 0)),
            pl.BlockSpec((tm, kl), lambda i, j: (i, 0)),
            pl.BlockSpec((kr, tn), lambda i, j: (0, j)),
            pl.BlockSpec((kl, tn), lambda i, j: (kr // kl, j)),
            pl.BlockSpec((tm, tn), lambda i, j: (i, j)),
            pl.BlockSpec((1, tn), lambda i, j: (0, j)),
        ],
        out_specs=[
            pl.BlockSpec((tm, tn), lambda i, j: (i, j)),
            pl.BlockSpec((tm, tn), lambda i, j: (i, j)),
            pl.BlockSpec((tm, LANES), lambda i, j: (i, 0)),
        ],
        out_shape=[
            jax.ShapeDtypeStruct((t, d), F32),
            jax.ShapeDtypeStruct((t, d), BF16),
            jax.ShapeDtypeStruct((t, LANES), F32),
        ],
        compiler_params=_params(("parallel", "arbitrary")),
        name="out_proj",
    )(ret, lru, w_out, w_out, x, g.reshape(1, d))


def _up_kernel(xg_ref, ssq_ref, w_ref, wn_ref, o_ref, wnb_ref, *, d_model):
    ms = jnp.sum(ssq_ref[...], axis=-1, keepdims=True) / d_model
    r = lax.rsqrt(ms + RMS_EPS)
    acc = jnp.dot(xg_ref[...], w_ref[...], preferred_element_type=F32)
    o_ref[...] = jnp.square(jnp.maximum(acc * r, 0.0)).astype(o_ref.dtype)
    wnb_ref[...] = wn_ref[...].astype(BF16)


def _up(xg, ssq, w, w_next, *, tm, tn):
    t, d = xg.shape
    n = w.shape[1]
    ni, nj = t // tm, n // tn
    wn_rows, wn_cols = w_next.shape
    sr = _slab_rows(wn_rows, ni * nj)
    slab = pl.BlockSpec((sr, wn_cols), lambda i, j: (i * nj + j, 0))
    return pl.pallas_call(
        functools.partial(_up_kernel, d_model=d),
        grid=(ni, nj),
        in_specs=[
            pl.BlockSpec((tm, d), lambda i, j: (i, 0)),
            pl.BlockSpec((tm, LANES), lambda i, j: (i, 0)),
            pl.BlockSpec((d, tn), lambda i, j: (0, j)),
            slab,
        ],
        out_specs=[pl.BlockSpec((tm, tn), lambda i, j: (i, j)), slab],
        out_shape=[jax.ShapeDtypeStruct((t, n), BF16),
                   jax.ShapeDtypeStruct((wn_rows, wn_cols), BF16)],
        compiler_params=_params(("parallel", "arbitrary")),
        name="mlp_up",
    )(xg, ssq, w, w_next)


def _down_norm_kernel(a_ref, w_ref, x_ref, g_ref, o_ref, r_ref, *, tn):
    kk = pl.program_id(1)
    d = o_ref.shape[1]
    xs = x_ref.shape[1]

    last = pl.num_programs(1) - 1
    lanes = pl.ds(pl.multiple_of(kk * xs, xs), xs)

    def panels(first, final):
        if not first:
            o_ref[:, lanes] += x_ref[...]
        ssq = None
        for c in range(d // tn):
            cols = slice(c * tn, (c + 1) * tn)
            val = jnp.dot(a_ref[...], w_ref[:, cols], preferred_element_type=F32)
            if not first:
                val = o_ref[:, cols] + val
            o_ref[:, cols] = val
            if final:
                sq = val * val
                for l in range(tn // LANES):
                    blk = sq[:, l * LANES:(l + 1) * LANES]
                    ssq = blk if ssq is None else ssq + blk
        if first:
            o_ref[:, lanes] += x_ref[...]
        if final:
            ms = jnp.sum(ssq, axis=-1, keepdims=True) / d
            r_ref[...] = jnp.broadcast_to(lax.rsqrt(ms + RMS_EPS), r_ref.shape)

    pl.when(kk == 0)(functools.partial(panels, True, False))
    pl.when(jnp.logical_and(kk != 0, kk != last))(functools.partial(panels, False, False))
    pl.when(kk == last)(functools.partial(panels, False, True))

    @pl.when(kk == last)
    def _():
        def apply_scale(r, carry):
            rows = pl.ds(pl.multiple_of(r * NORM_ROWS, NORM_ROWS), NORM_ROWS)
            scale = r_ref[rows, :]
            for l in range(d // LANES):
                cols = slice(l * LANES, (l + 1) * LANES)
                o_ref[rows, cols] = o_ref[rows, cols] * scale * g_ref[:, cols]
            return carry

        lax.fori_loop(0, o_ref.shape[0] // NORM_ROWS, apply_scale, 0, unroll=2)


def _down_norm(act, w_down, x, g, *, tm, tk, tn):
    t, d = x.shape
    f = act.shape[1]
    nk = f // tk
    assert nk > 1 and d % nk == 0 and (d // nk) % LANES == 0
    return pl.pallas_call(
        functools.partial(_down_norm_kernel, tn=tn),
        grid=(t // tm, nk),
        in_specs=[
            pl.BlockSpec((tm, tk), lambda i, k: (i, k)),
            pl.BlockSpec((tk, d), lambda i, k: (k, 0)),
            pl.BlockSpec((tm, d // nk), lambda i, k: (i, k)),
            pl.BlockSpec((1, d), lambda i, k: (0, 0)),
        ],
        out_specs=pl.BlockSpec((tm, d), lambda i, k: (i, 0)),
        out_shape=jax.ShapeDtypeStruct((t, d), F32),
        scratch_shapes=[pltpu.VMEM((tm, LANES), F32)],
        compiler_params=_params(("parallel", "arbitrary")),
        name="down_norm",
    )(act, w_down, x, g.reshape(1, d))


def kernel(x, norm1_g, w_in, ret_gn_g, conv_w, conv_b, gate_a_w, gate_a_b, gate_x_w, gate_x_b,
           lru_lambda, lru_norm_g, w_out, norm2_g, w_up, w_down, normf_g):
    b, s, d = x.shape
    assert w_in.shape[0] == 1, "single-layer problem"
    cos, sin = _rope_table(s, RET_HEAD_DIM // 2)
    xt = x.reshape(b * s, d)
    ret_width = RET_HEADS * RET_HEAD_DIM
    qkv, rest = _in_proj(xt, norm1_g[0], w_in[0].astype(BF16), cos, sin, tm=1024, tn=1024,
                         n_norm=8, ret_width=ret_width,
                         lru_width=LRU_BLOCKS * LRU_BLOCK_DIM, head_dim=RET_HEAD_DIM)
    qkv = qkv.reshape(b, s, -1)
    rest = rest.reshape(b, s, -1)
    ret, w_out_b = _retention(qkv, rest, ret_gn_g[0], w_out[0], chunk=256, rows=1024, group=2)
    wg = jnp.concatenate([gate_a_w[0], gate_x_w[0]], axis=-1).astype(BF16)
    lru, w_up_b = _rglru(rest, conv_w[0], conv_b[0], wg, gate_a_b[0], gate_x_b[0],
                         lru_lambda[0], lru_norm_g[0], w_up[0], tc=256)
    x1, xg, ssq = _out_proj(ret.reshape(b * s, -1), lru.reshape(b * s, -1),
                            w_out_b, xt, norm2_g[0], tm=1024, tn=512)
    act, w_down_b = _up(xg, ssq, w_up_b, w_down[0], tm=1024, tn=1024)
    out = _down_norm(act, w_down_b, x1, normf_g, tm=1024, tk=1024, tn=1024)
    return out.reshape(b, s, d)
```

```python
import functools
import math

import jax
import jax.numpy as jnp
from jax import lax
from jax.experimental import pallas as pl
from jax.experimental.pallas import tpu as pltpu

RET_HEADS = 8
RET_HEAD_DIM = 256
LRU_BLOCKS = 16
LRU_BLOCK_DIM = 128
CONV_WIDTH = 4
ROPE_BASE = 10000.0
LRU_C = 8.0
LOG2_E = math.log2(math.e)
RMS_EPS = 1e-6
GN_EPS = 1e-5

SUBLANES = 8
LANES = 128
V7X_VMEM_LIMIT_BYTES = 60 * 1024 * 1024
NORM_ROWS = 4 * SUBLANES
SCAN_SEG = 4
SCAN_ROWS = SCAN_SEG * SUBLANES

F32 = jnp.float32
BF16 = jnp.bfloat16


def _params(semantics):
    return pltpu.CompilerParams(dimension_semantics=semantics,
                                vmem_limit_bytes=V7X_VMEM_LIMIT_BYTES)


def _rope_kernel(cos_ref, sin_ref, *, rows, half):
    r0 = pl.program_id(0) * rows
    pos = (lax.broadcasted_iota(jnp.int32, (rows, half), 0) + r0).astype(F32)
    idx = lax.broadcasted_iota(jnp.int32, (rows, half), 1).astype(F32)
    inv = jnp.exp(-(idx / half) * jnp.log(F32(ROPE_BASE)))
    ang = pos * inv
    cos_ref[...] = jnp.cos(ang)
    sin_ref[...] = jnp.sin(ang)


def _rope_table(seq, half, rows=512):
    return pl.pallas_call(
        functools.partial(_rope_kernel, rows=rows, half=half),
        grid=(seq // rows,),
        out_specs=[pl.BlockSpec((rows, half), lambda i: (i, 0))] * 2,
        out_shape=[jax.ShapeDtypeStruct((seq, half), F32)] * 2,
        compiler_params=_params(("parallel",)),
        name="rope_table",
    )()


def _in_proj_kernel(x_ref, g_ref, w_ref, cos_ref, sin_ref, qkv_ref, rest_ref, h_ref, *,
                    n_norm, bounds, head_dim):
    i = pl.program_id(0)
    j = pl.program_id(1)
    half = head_dim // 2
    chunk = x_ref.shape[0]
    parity = i % 2

    def norm_ahead():
        x = x_ref[...]
        ms = jnp.mean(x * x, axis=-1, keepdims=True)
        rows = pl.ds(pl.multiple_of(j * chunk, chunk), chunk)
        h_ref[parity, rows, :] = (x * lax.rsqrt(ms + RMS_EPS) * g_ref[...]).astype(BF16)

    def tile(kind, with_norm):
        acc = jnp.dot(h_ref[1 - parity], w_ref[...], preferred_element_type=F32)
        if kind == "rotary":
            cos = cos_ref[...]
            sin = sin_ref[...]
            for hd in range(acc.shape[1] // head_dim):
                lo = slice(hd * head_dim, hd * head_dim + half)
                hi = slice(hd * head_dim + half, (hd + 1) * head_dim)
                t1, t2 = acc[:, lo], acc[:, hi]
                qkv_ref[:, lo] = (t1 * cos - t2 * sin).astype(qkv_ref.dtype)
                qkv_ref[:, hi] = (t2 * cos + t1 * sin).astype(qkv_ref.dtype)
        elif kind == "bf16":
            qkv_ref[...] = acc.astype(qkv_ref.dtype)
        elif kind == "gelu":
            rest_ref[...] = jax.nn.gelu(acc, approximate=True)
        else:
            rest_ref[...] = acc
        if with_norm:
            norm_ahead()

    pl.when(jnp.logical_and(i == 0, j < n_norm))(norm_ahead)

    kinds = ("rotary", "bf16", "f32", "gelu")
    for kind, lo, hi in zip(kinds, bounds[:-1], bounds[1:]):
        for with_norm, a, b in ((True, lo, min(hi, n_norm)), (False, max(lo, n_norm), hi)):
            if a < b:
                cond = jnp.logical_and(i > 0, jnp.logical_and(j >= a, j < b))
                pl.when(cond)(functools.partial(tile, kind, with_norm))


def _in_proj(x, g, w, cos, sin, *, tm, tn, n_norm, ret_width, lru_width, head_dim):
    t, d = x.shape
    n = w.shape[1]
    seq = cos.shape[0]
    assert n == 4 * ret_width + 2 * lru_width
    assert ret_width % tn == 0 and lru_width % tn == 0 and tn % head_dim == 0
    assert seq % tm == 0
    rw, lw = ret_width // tn, lru_width // tn
    bounds = (0, 2 * rw, 3 * rw, 4 * rw + lw, 4 * rw + 2 * lw)
    n_bf16 = 3 * rw
    bf16_width = 3 * ret_width
    ni = t // tm
    pos_blocks = seq // tm
    chunk = tm // n_norm
    assert n_norm <= n // tn and tm % n_norm == 0 and chunk % (2 * SUBLANES) == 0

    def mat_block(i):
        return jnp.maximum(i - 1, 0)

    def x_index(i, j):
        return jnp.minimum(i, ni - 1) * n_norm + jnp.minimum(j, n_norm - 1), 0

    def first(i, blk):
        return jnp.where(i == 0, 0, blk)

    return pl.pallas_call(
        functools.partial(_in_proj_kernel, n_norm=n_norm, bounds=bounds, head_dim=head_dim),
        grid=(ni + 1, n // tn),
        in_specs=[
            pl.BlockSpec((chunk, d), x_index),
            pl.BlockSpec((1, d), lambda i, j: (0, 0)),
            pl.BlockSpec((d, tn), lambda i, j: (0, first(i, j))),
            pl.BlockSpec((tm, head_dim // 2), lambda i, j: (mat_block(i) % pos_blocks, 0)),
            pl.BlockSpec((tm, head_dim // 2), lambda i, j: (mat_block(i) % pos_blocks, 0)),
        ],
        out_specs=[
            pl.BlockSpec((tm, tn),
                         lambda i, j: (mat_block(i), first(i, jnp.minimum(j, n_bf16 - 1)))),
            pl.BlockSpec((tm, tn),
                         lambda i, j: (mat_block(i), first(i, jnp.maximum(j - n_bf16, 0)))),
        ],
        out_shape=[jax.ShapeDtypeStruct((t, bf16_width), BF16),
                   jax.ShapeDtypeStruct((t, n - bf16_width), F32)],
        scratch_shapes=[pltpu.VMEM((2, tm, d), BF16)],
        compiler_params=_params(("arbitrary", "arbitrary")),
        name="in_proj",
    )(x, g.reshape(1, d), w, cos, sin)


def _slab_rows(w_rows, steps):
    rows, rem = divmod(w_rows, steps)
    assert rem == 0 and rows % (2 * SUBLANES) == 0, (w_rows, steps)
    return rows


def _retention_kernel(q_ref, k_ref, v_ref, g_ref, gn_ref, wn_ref, o_ref, wnb_ref,
                      state_ref, dec_ref, qd_ref, kd_ref, cd_ref, *, chunk, n_sub, head_dim):
    scale = head_dim ** -0.5
    assert math.frexp(scale)[0] == 0.5
    group = state_ref.shape[0]

    @pl.when(pl.program_id(2) == 0)
    def _():
        state_ref[...] = jnp.zeros_like(state_ref)
        row = lax.broadcasted_iota(jnp.int32, (chunk, chunk), 0)
        col = lax.broadcasted_iota(jnp.int32, (chunk, chunk), 1)
        diff = (row - col).astype(F32)
        idx = lax.broadcasted_iota(jnp.int32, (chunk, head_dim), 0).astype(F32)
        for hh in range(group):
            head = pl.program_id(1) * group + hh
            hf = jnp.full((1, 1), head, jnp.int32).astype(F32)
            log_g = jnp.log1p(-jnp.exp2(-5.0 - hf))
            dec_ref[hh] = jnp.where(diff >= 0, jnp.exp(log_g * jnp.maximum(diff, 0.0)),
                                    0.0) * scale
            qd_ref[hh] = jnp.exp(log_g * (idx + 1.0))
            kd_ref[hh] = jnp.exp(log_g * (chunk - 1.0 - idx)) * scale
            cd_ref[hh] = jnp.broadcast_to(jnp.exp(log_g * chunk), cd_ref.shape[1:])

    wnb_ref[...] = wn_ref[...].astype(BF16)

    states = [state_ref[hh] for hh in range(group)]
    for c in range(n_sub):
        rows = pl.ds(c * chunk, chunk)
        for hh in range(group):
            cols = slice(hh * head_dim, (hh + 1) * head_dim)
            qb = q_ref[0, rows, cols]
            kb = k_ref[0, rows, cols]
            vb = v_ref[0, rows, cols]
            scores = lax.dot_general(qb, kb, (((1,), (1,)), ((), ())),
                                     preferred_element_type=F32) * dec_ref[hh]
            out = (jnp.dot(scores.astype(BF16), vb, preferred_element_type=F32)
                   + qd_ref[hh] * jnp.dot(qb, states[hh].astype(BF16),
                                          preferred_element_type=F32))
            kd = (kb.astype(F32) * kd_ref[hh]).astype(BF16)
            kv = lax.dot_general(kd, vb, (((0,), (0,)), ((), ())), preferred_element_type=F32)
            states[hh] = cd_ref[hh] * states[hh] + kv

            mu = jnp.mean(out, axis=-1, keepdims=True)
            cen = out - mu
            var = jnp.mean(cen * cen, axis=-1, keepdims=True)
            o = cen * lax.rsqrt(var + GN_EPS) * gn_ref[:, cols]
            o_ref[0, rows, cols] = (jax.nn.silu(g_ref[0, rows, cols]) * o).astype(o_ref.dtype)
    for hh in range(group):
        state_ref[hh] = states[hh]


def _retention(qkv, rest, gn_g, w_next, *, chunk, rows, group):
    b, s, _ = qkv.shape
    h, d = RET_HEADS, RET_HEAD_DIM
    assert h % group == 0
    hg, gd = h // group, group * d
    nc = s // rows
    wn_rows, wn_cols = w_next.shape
    sr = _slab_rows(wn_rows, b * hg * nc)
    slab = pl.BlockSpec((sr, wn_cols), lambda bi, hi, ci: ((bi * hg + hi) * nc + ci, 0))

    def col(off):
        return pl.BlockSpec((1, rows, gd), lambda bi, hi, ci: (bi, ci, off + hi))

    return pl.pallas_call(
        functools.partial(_retention_kernel, chunk=chunk, n_sub=rows // chunk, head_dim=d),
        grid=(b, hg, nc),
        in_specs=[
            col(0), col(hg), col(2 * hg), col(0),
            pl.BlockSpec((1, gd), lambda bi, hi, ci: (0, hi)),
            slab,
        ],
        out_specs=[pl.BlockSpec((1, rows, gd), lambda bi, hi, ci: (bi, ci, hi)), slab],
        out_shape=[jax.ShapeDtypeStruct((b, s, h * d), BF16),
                   jax.ShapeDtypeStruct((wn_rows, wn_cols), BF16)],
        scratch_shapes=[
            pltpu.VMEM((group, d, d), F32),
            pltpu.VMEM((group, chunk, chunk), F32),
            pltpu.VMEM((group, chunk, d), F32),
            pltpu.VMEM((group, chunk, d), F32),
            pltpu.VMEM((group, 1, d), F32),
        ],
        compiler_params=_params(("parallel", "parallel", "arbitrary")),
        name="retention",
    )(qkv, qkv, qkv, rest, gn_g.reshape(1, h * d), w_next)


def _rglru_kernel(xr_ref, yr_ref, cw_ref, cb_ref, wg_ref, ba_ref, bx_ref, lam_ref, ng_ref,
                  wn_ref, o_ref, wnb_ref, ext_ref, a_ref, u_ref, h_ref, *, tc):
    @pl.when(pl.program_id(1) == 0)
    def _():
        ext_ref[0:SUBLANES, :] = jnp.zeros((SUBLANES, ext_ref.shape[1]), F32)
        h_ref[...] = jnp.zeros_like(h_ref)

    wnb_ref[...] = wn_ref[...].astype(BF16)

    ext_ref[SUBLANES:SUBLANES + tc, :] = xr_ref[0]
    xe = ext_ref[...]
    acc = cw_ref[0:1, :] * xe
    for j in range(1, CONV_WIDTH):
        acc = cw_ref[j:j + 1, :] * xe + pltpu.roll(acc, 1, axis=0)
    xc = acc[SUBLANES:, :] + cb_ref[...]
    ext_ref[0:SUBLANES, :] = ext_ref[tc:tc + SUBLANES, :]

    sp = jax.nn.softplus(-lam_ref[...])
    k_tanh = LRU_C * sp
    k_exp2 = -(LRU_C * LOG2_E) * sp
    for n in range(LRU_BLOCKS):
        sl = slice(n * LRU_BLOCK_DIM, (n + 1) * LRU_BLOCK_DIM)
        xb = xc[:, sl]
        gates = jnp.dot(xb.astype(BF16), wg_ref[n], preferred_element_type=F32)
        r = jax.nn.sigmoid(gates[:, :LRU_BLOCK_DIM] + ba_ref[:, sl])
        i = jax.nn.sigmoid(gates[:, LRU_BLOCK_DIM:] + bx_ref[:, sl])
        a = jnp.exp2(r * k_exp2[:, sl])
        a_ref[n] = a
        one_minus_a2 = jnp.tanh(r * k_tanh[:, sl]) * (1.0 + a * a)
        root = jnp.where(one_minus_a2 > 0.0, one_minus_a2 * lax.rsqrt(one_minus_a2), 0.0)
        u_ref[n] = root * (i * xb)

    sub = lax.broadcasted_iota(jnp.int32, (SUBLANES, LRU_BLOCK_DIM), 0)
    for grp in range(tc // SCAN_ROWS):
        base = grp * SCAN_ROWS
        for n in range(LRU_BLOCKS):
            def seg_rows(jj):
                return pl.ds(base + jj, SUBLANES, stride=SCAN_SEG)

            hs = [u_ref[n, seg_rows(0), :]]
            ps = [a_ref[n, seg_rows(0), :]]
            for jj in range(1, SCAN_SEG):
                a_j = a_ref[n, seg_rows(jj), :]
                hs.append(a_j * hs[-1] + u_ref[n, seg_rows(jj), :])
                ps.append(a_j * ps[-1])
            p_end, h_end = ps[-1], hs[-1]
            shift = 1
            while shift < SUBLANES:
                keep = sub >= shift
                h_prev = pltpu.roll(h_end, shift, axis=0)
                p_prev = pltpu.roll(p_end, shift, axis=0)
                h_end = jnp.where(keep, p_end * h_prev + h_end, h_end)
                p_end = jnp.where(keep, p_end * p_prev, p_end)
                shift *= 2
            h0 = jnp.broadcast_to(h_ref[n, SUBLANES - 1:SUBLANES, :], h_end.shape)
            end_state = h_end + p_end * h0
            h_ref[n] = end_state
            start = jnp.where(sub >= 1, pltpu.roll(end_state, 1, axis=0), h0)
            for jj in range(SCAN_SEG):
                u_ref[n, seg_rows(jj), :] = hs[jj] + ps[jj] * start

    ssq = None
    for n in range(LRU_BLOCKS):
        sl = slice(n * LRU_BLOCK_DIM, (n + 1) * LRU_BLOCK_DIM)
        y = u_ref[n] * yr_ref[0, :, sl]
        u_ref[n] = y
        ssq = y * y if ssq is None else ssq + y * y
    ms = jnp.sum(ssq, axis=-1, keepdims=True) / (LRU_BLOCKS * LRU_BLOCK_DIM)
    scale = jnp.broadcast_to(lax.rsqrt(ms + RMS_EPS), ssq.shape)
    for n in range(LRU_BLOCKS):
        sl = slice(n * LRU_BLOCK_DIM, (n + 1) * LRU_BLOCK_DIM)
        o_ref[0, :, sl] = (u_ref[n] * scale * ng_ref[:, sl]).astype(o_ref.dtype)


def _rglru(proj, conv_w, conv_b, wg, ba, bx, lam, ng, w_next, *, tc):
    b, s, pw = proj.shape
    w = LRU_BLOCKS * LRU_BLOCK_DIM
    xr_blk = (pw - 2 * w) // w
    nt = s // tc
    wn_rows, wn_cols = w_next.shape
    sr = _slab_rows(wn_rows, b * nt)
    slab = pl.BlockSpec((sr, wn_cols), lambda bi, ti: (bi * nt + ti, 0))
    row = lambda a: a.reshape(1, w)
    vec = pl.BlockSpec((1, w), lambda bi, ti: (0, 0))
    return pl.pallas_call(
        functools.partial(_rglru_kernel, tc=tc),
        grid=(b, nt),
        in_specs=[
            pl.BlockSpec((1, tc, w), lambda bi, ti: (bi, ti, xr_blk)),
            pl.BlockSpec((1, tc, w), lambda bi, ti: (bi, ti, xr_blk + 1)),
            pl.BlockSpec((CONV_WIDTH, w), lambda bi, ti: (0, 0)),
            vec,
            pl.BlockSpec((LRU_BLOCKS, LRU_BLOCK_DIM, 2 * LRU_BLOCK_DIM),
                         lambda bi, ti: (0, 0, 0)),
            vec, vec, vec, vec,
            slab,
        ],
        out_specs=[pl.BlockSpec((1, tc, w), lambda bi, ti: (bi, ti, 0)), slab],
        out_shape=[jax.ShapeDtypeStruct((b, s, w), BF16),
                   jax.ShapeDtypeStruct((wn_rows, wn_cols), BF16)],
        scratch_shapes=[
            pltpu.VMEM((tc + SUBLANES, w), F32),
            pltpu.VMEM((LRU_BLOCKS, tc, LRU_BLOCK_DIM), F32),
            pltpu.VMEM((LRU_BLOCKS, tc, LRU_BLOCK_DIM), F32),
            pltpu.VMEM((LRU_BLOCKS, SUBLANES, LRU_BLOCK_DIM), F32),
        ],
        compiler_params=_params(("parallel", "arbitrary")),
        name="rglru",
    )(proj, proj, conv_w, row(conv_b), wg, row(ba), row(bx), row(lam), row(ng), w_next)


def _out_proj_kernel(ret_ref, lru_ref, wr_ref, wl_ref, x_ref, g_ref, o_ref, xg_ref, ssq_ref):
    @pl.when(pl.program_id(1) == 0)
    def _():
        ssq_ref[...] = jnp.zeros_like(ssq_ref)

    acc = jnp.dot(ret_ref[...], wr_ref[...], preferred_element_type=F32)
    acc = acc + jnp.dot(lru_ref[...], wl_ref[...], preferred_element_type=F32)
    x1 = x_ref[...] + acc
    o_ref[...] = x1
    xg_ref[...] = (x1 * g_ref[...]).astype(BF16)
    sq = x1 * x1
    part = sq[:, :LANES]
    for c in range(1, sq.shape[1] // LANES):
        part = part + sq[:, c * LANES:(c + 1) * LANES]
    ssq_ref[...] += part


def _out_proj(ret, lru, w_out, x, g, *, tm, tn):
    t, d = x.shape
    kr, kl = ret.shape[1], lru.shape[1]
    return pl.pallas_call(
        _out_proj_kernel,
        grid=(t // tm, d // tn),
        in_specs=[
            pl.BlockSpec((tm, kr), lambda i, j: (i, 0)),
            pl.BlockSpec((tm, kl), lambda i, j: (i, 0)),
            pl.BlockSpec((kr, tn), lambda i, j: (0, j)),
            pl.BlockSpec((kl, tn), lambda i, j: (kr // kl, j)),
            pl.BlockSpec((tm, tn), lambda i, j: (i, j)),
            pl.BlockSpec((1, tn), lambda i, j: (0, j)),
        ],
        out_specs=[
            pl.BlockSpec((tm, tn), lambda i, j: (i, j)),
            pl.BlockSpec((tm, tn), lambda i, j: (i, j)),
            pl.BlockSpec((tm, LANES), lambda i, j: (i, 0)),
        ],
        out_shape=[
            jax.ShapeDtypeStruct((t, d), F32),
            jax.ShapeDtypeStruct((t, d), BF16),
            jax.ShapeDtypeStruct((t, LANES), F32),
        ],
        compiler_params=_params(("parallel", "arbitrary")),
        name="out_proj",
    )(ret, lru, w_out, w_out, x, g.reshape(1, d))


def _up_kernel(xg_ref, ssq_ref, w_ref, wn_ref, o_ref, wnb_ref, *, d_model):
    ms = jnp.sum(ssq_ref[...], axis=-1, keepdims=True) / d_model
    r = lax.rsqrt(ms + RMS_EPS)
    acc = jnp.dot(xg_ref[...], w_ref[...], preferred_element_type=F32)
    o_ref[...] = jnp.square(jnp.maximum(acc * r, 0.0)).astype(o_ref.dtype)
    wnb_ref[...] = wn_ref[...].astype(BF16)


def _up(xg, ssq, w, w_next, *, tm, tn):
    t, d = xg.shape
    n = w.shape[1]
    ni, nj = t // tm, n // tn
    wn_rows, wn_cols = w_next.shape
    sr = _slab_rows(wn_rows, ni * nj)
    slab = pl.BlockSpec((sr, wn_cols), lambda i, j: (i * nj + j, 0))
    return pl.pallas_call(
        functools.partial(_up_kernel, d_model=d),
        grid=(ni, nj),
        in_specs=[
            pl.BlockSpec((tm, d), lambda i, j: (i, 0)),
            pl.BlockSpec((tm, LANES), lambda i, j: (i, 0)),
            pl.BlockSpec((d, tn), lambda i, j: (0, j)),
            slab,
        ],
        out_specs=[pl.BlockSpec((tm, tn), lambda i, j: (i, j)), slab],
        out_shape=[jax.ShapeDtypeStruct((t, n), BF16),
                   jax.ShapeDtypeStruct((wn_rows, wn_cols), BF16)],
        compiler_params=_params(("parallel", "arbitrary")),
        name="mlp_up",
    )(xg, ssq, w, w_next)


def _down_norm_kernel(a_ref, w_ref, x_ref, g_ref, o_ref, r_ref, *, tn):
    kk = pl.program_id(1)
    d = o_ref.shape[1]
    xs = x_ref.shape[1]

    last = pl.num_programs(1) - 1
    lanes = pl.ds(pl.multiple_of(kk * xs, xs), xs)

    def panels(first, final):
        if not first:
            o_ref[:, lanes] += x_ref[...]
        ssq = None
        for c in range(d // tn):
            cols = slice(c * tn, (c + 1) * tn)
            val = jnp.dot(a_ref[...], w_ref[:, cols], preferred_element_type=F32)
            if not first:
                val = o_ref[:, cols] + val
            o_ref[:, cols] = val
            if final:
                sq = val * val
                for l in range(tn // LANES):
                    blk = sq[:, l * LANES:(l + 1) * LANES]
                    ssq = blk if ssq is None else ssq + blk
        if first:
            o_ref[:, lanes] += x_ref[...]
        if final:
            ms = jnp.sum(ssq, axis=-1, keepdims=True) / d
            r_ref[...] = jnp.broadcast_to(lax.rsqrt(ms + RMS_EPS), r_ref.shape)

    pl.when(kk == 0)(functools.partial(panels, True, False))
    pl.when(jnp.logical_and(kk != 0, kk != last))(functools.partial(panels, False, False))
    pl.when(kk == last)(functools.partial(panels, False, True))

    @pl.when(kk == last)
    def _():
        def apply_scale(r, carry):
            rows = pl.ds(pl.multiple_of(r * NORM_ROWS, NORM_ROWS), NORM_ROWS)
            scale = r_ref[rows, :]
            for l in range(d // LANES):
                cols = slice(l * LANES, (l + 1) * LANES)
                o_ref[rows, cols] = o_ref[rows, cols] * scale * g_ref[:, cols]
            return carry

        lax.fori_loop(0, o_ref.shape[0] // NORM_ROWS, apply_scale, 0, unroll=2)


def _down_norm(act, w_down, x, g, *, tm, tk, tn):
    t, d = x.shape
    f = act.shape[1]
    nk = f // tk
    assert nk > 1 and d % nk == 0 and (d // nk) % LANES == 0
    return pl.pallas_call(
        functools.partial(_down_norm_kernel, tn=tn),
        grid=(t // tm, nk),
        in_specs=[
            pl.BlockSpec((tm, tk), lambda i, k: (i, k)),
            pl.BlockSpec((tk, d), lambda i, k: (k, 0)),
            pl.BlockSpec((tm, d // nk), lambda i, k: (i, k)),
            pl.BlockSpec((1, d), lambda i, k: (0, 0)),
        ],
        out_specs=pl.BlockSpec((tm, d), lambda i, k: (i, 0)),
        out_shape=jax.ShapeDtypeStruct((t, d), F32),
        scratch_shapes=[pltpu.VMEM((tm, LANES), F32)],
        compiler_params=_params(("parallel", "arbitrary")),
        name="down_norm",
    )(act, w_down, x, g.reshape(1, d))


def kernel(x, norm1_g, w_in, ret_gn_g, conv_w, conv_b, gate_a_w, gate_a_b, gate_x_w, gate_x_b,
           lru_lambda, lru_norm_g, w_out, norm2_g, w_up, w_down, normf_g):
    b, s, d = x.shape
    assert w_in.shape[0] == 1, "single-layer problem"
    cos, sin = _rope_table(s, RET_HEAD_DIM // 2)
    xt = x.reshape(b * s, d)
    ret_width = RET_HEADS * RET_HEAD_DIM
    qkv, rest = _in_proj(xt, norm1_g[0], w_in[0].astype(BF16), cos, sin, tm=1024, tn=1024,
                         n_norm=8, ret_width=ret_width,
                         lru_width=LRU_BLOCKS * LRU_BLOCK_DIM, head_dim=RET_HEAD_DIM)
    qkv = qkv.reshape(b, s, -1)
    rest = rest.reshape(b, s, -1)
    ret, w_out_b = _retention(qkv, rest, ret_gn_g[0], w_out[0], chunk=256, rows=1024, group=2)
    wg = jnp.concatenate([gate_a_w[0], gate_x_w[0]], axis=-1).astype(BF16)
    lru, w_up_b = _rglru(rest, conv_w[0], conv_b[0], wg, gate_a_b[0], gate_x_b[0],
                         lru_lambda[0], lru_norm_g[0], w_up[0], tc=256)
    x1, xg, ssq = _out_proj(ret.reshape(b * s, -1), lru.reshape(b * s, -1),
                            w_out_b, xt, norm2_g[0], tm=1024, tn=512)
    act, w_down_b = _up(xg, ssq, w_up_b, w_down[0], tm=1024, tn=1024)
    out = _down_norm(act, w_down_b, x1, normf_g, tm=1024, tk=1024, tn=1024)
    return out.reshape(b, s, d)
```

```python
import functools
import math

import jax
import jax.numpy as jnp
from jax import lax
from jax.experimental import pallas as pl
from jax.experimental.pallas import tpu as pltpu

RET_HEADS = 8
RET_HEAD_DIM = 256
LRU_BLOCKS = 16
LRU_BLOCK_DIM = 128
CONV_WIDTH = 4
ROPE_BASE = 10000.0
LRU_C = 8.0
LOG2_E = math.log2(math.e)
RMS_EPS = 1e-6
GN_EPS = 1e-5

SUBLANES = 8
LANES = 128
V7X_VMEM_LIMIT_BYTES = 60 * 1024 * 1024
NORM_ROWS = 4 * SUBLANES
SCAN_SEG = 4
SCAN_ROWS = SCAN_SEG * SUBLANES

F32 = jnp.float32
BF16 = jnp.bfloat16


def _params(semantics):
    return pltpu.CompilerParams(dimension_semantics=semantics,
                                vmem_limit_bytes=V7X_VMEM_LIMIT_BYTES)


def _rope_kernel(cos_ref, sin_ref, *, rows, half):
    r0 = pl.program_id(0) * rows
    pos = (lax.broadcasted_iota(jnp.int32, (rows, half), 0) + r0).astype(F32)
    idx = lax.broadcasted_iota(jnp.int32, (rows, half), 1).astype(F32)
    inv = jnp.exp(-(idx / half) * jnp.log(F32(ROPE_BASE)))
    ang = pos * inv
    cos_ref[...] = jnp.cos(ang)
    sin_ref[...] = jnp.sin(ang)


def _rope_table(seq, half, rows=512):
    return pl.pallas_call(
        functools.partial(_rope_kernel, rows=rows, half=half),
        grid=(seq // rows,),
        out_specs=[pl.BlockSpec((rows, half), lambda i: (i, 0))] * 2,
        out_shape=[jax.ShapeDtypeStruct((seq, half), F32)] * 2,
        compiler_params=_params(("parallel",)),
        name="rope_table",
    )()


def _in_proj_kernel(x_ref, g_ref, w_ref, cos_ref, sin_ref, qkv_ref, rest_ref, h_ref, *,
                    n_norm, bounds, head_dim):
    i = pl.program_id(0)
    j = pl.program_id(1)
    half = head_dim // 2
    chunk = x_ref.shape[0]
    parity = i % 2

    def norm_ahead():
        x = x_ref[...]
        ms = jnp.mean(x * x, axis=-1, keepdims=True)
        rows = pl.ds(pl.multiple_of(j * chunk, chunk), chunk)
        h_ref[parity, rows, :] = (x * lax.rsqrt(ms + RMS_EPS) * g_ref[...]).astype(BF16)

    def tile(kind, with_norm):
        acc = jnp.dot(h_ref[1 - parity], w_ref[...], preferred_element_type=F32)
        if kind == "rotary":
            cos = cos_ref[...]
            sin = sin_ref[...]
            for hd in range(acc.shape[1] // head_dim):
                lo = slice(hd * head_dim, hd * head_dim + half)
                hi = slice(hd * head_dim + half, (hd + 1) * head_dim)
                t1, t2 = acc[:, lo], acc[:, hi]
                qkv_ref[:, lo] = (t1 * cos - t2 * sin).astype(qkv_ref.dtype)
                qkv_ref[:, hi] = (t2 * cos + t1 * sin).astype(qkv_ref.dtype)
        elif kind == "bf16":
            qkv_ref[...] = acc.astype(qkv_ref.dtype)
        elif kind == "gelu":
            rest_ref[...] = jax.nn.gelu(acc, approximate=True)
        else:
            rest_ref[...] = acc
        if with_norm:
            norm_ahead()

    pl.when(jnp.logical_and(i == 0, j < n_norm))(norm_ahead)

    kinds = ("rotary", "bf16", "f32", "gelu")
    for kind, lo, hi in zip(kinds, bounds[:-1], bounds[1:]):
        for with_norm, a, b in ((True, lo, min(hi, n_norm)), (False, max(lo, n_norm), hi)):
            if a < b:
                cond = jnp.logical_and(i > 0, jnp.logical_and(j >= a, j < b))
                pl.when(cond)(functools.partial(tile, kind, with_norm))


def _in_proj(x, g, w, cos, sin, *, tm, tn, n_norm, ret_width, lru_width, head_dim):
    t, d = x.shape
    n = w.shape[1]
    seq = cos.shape[0]
    assert n == 4 * ret_width + 2 * lru_width
    assert ret_width % tn == 0 and lru_width % tn == 0 and tn % head_dim == 0
    assert seq % tm == 0
    rw, lw = ret_width // tn, lru_width // tn
    bounds = (0, 2 * rw, 3 * rw, 4 * rw + lw, 4 * rw + 2 * lw)
    n_bf16 = 3 * rw
    bf16_width = 3 * ret_width
    ni = t // tm
    pos_blocks = seq // tm
    chunk = tm // n_norm
    assert n_norm <= n // tn and tm % n_norm == 0 and chunk % (2 * SUBLANES) == 0

    def mat_block(i):
        return jnp.maximum(i - 1, 0)

    def x_index(i, j):
        return jnp.minimum(i, ni - 1) * n_norm + jnp.minimum(j, n_norm - 1), 0

    def first(i, blk):
        return jnp.where(i == 0, 0, blk)

    return pl.pallas_call(
        functools.partial(_in_proj_kernel, n_norm=n_norm, bounds=bounds, head_dim=head_dim),
        grid=(ni + 1, n // tn),
        in_specs=[
            pl.BlockSpec((chunk, d), x_index),
            pl.BlockSpec((1, d), lambda i, j: (0, 0)),
            pl.BlockSpec((d, tn), lambda i, j: (0, first(i, j))),
            pl.BlockSpec((tm, head_dim // 2), lambda i, j: (mat_block(i) % pos_blocks, 0)),
            pl.BlockSpec((tm, head_dim // 2), lambda i, j: (mat_block(i) % pos_blocks, 0)),
        ],
        out_specs=[
            pl.BlockSpec((tm, tn),
                         lambda i, j: (mat_block(i), first(i, jnp.minimum(j, n_bf16 - 1)))),
            pl.BlockSpec((tm, tn),
                         lambda i, j: (mat_block(i), first(i, jnp.maximum(j - n_bf16, 0)))),
        ],
        out_shape=[jax.ShapeDtypeStruct((t, bf16_width), BF16),
                   jax.ShapeDtypeStruct((t, n - bf16_width), F32)],
        scratch_shapes=[pltpu.VMEM((2, tm, d), BF16)],
        compiler_params=_params(("arbitrary", "arbitrary")),
        name="in_proj",
    )(x, g.reshape(1, d), w, cos, sin)


def _slab_rows(w_rows, steps):
    rows, rem = divmod(w_rows, steps)
    assert rem == 0 and rows % (2 * SUBLANES) == 0, (w_rows, steps)
    return rows


def _retention_kernel(q_ref, k_ref, v_ref, g_ref, gn_ref, wn_ref, o_ref, wnb_ref,
                      state_ref, dec_ref, qd_ref, kd_ref, cd_ref, *, chunk, n_sub, head_dim):
    scale = head_dim ** -0.5
    assert math.frexp(scale)[0] == 0.5
    group = state_ref.shape[0]

    @pl.when(pl.program_id(2) == 0)
    def _():
        state_ref[...] = jnp.zeros_like(state_ref)
        row = lax.broadcasted_iota(jnp.int32, (chunk, chunk), 0)
        col = lax.broadcasted_iota(jnp.int32, (chunk, chunk), 1)
        diff = (row - col).astype(F32)
        idx = lax.broadcasted_iota(jnp.int32, (chunk, head_dim), 0).astype(F32)
        for hh in range(group):
            head = pl.program_id(1) * group + hh
            hf = jnp.full((1, 1), head, jnp.int32).astype(F32)
            log_g = jnp.log1p(-jnp.exp2(-5.0 - hf))
            dec_ref[hh] = jnp.where(diff >= 0, jnp.exp(log_g * jnp.maximum(diff, 0.0)),
                                    0.0) * scale
            qd_ref[hh] = jnp.exp(log_g * (idx + 1.0))
            kd_ref[hh] = jnp.exp(log_g * (chunk - 1.0 - idx)) * scale
            cd_ref[hh] = jnp.broadcast_to(jnp.exp(log_g * chunk), cd_ref.shape[1:])

    wnb_ref[...] = wn_ref[...].astype(BF16)

    states = [state_ref[hh] for hh in range(group)]
    for c in range(n_sub):
        rows = pl.ds(c * chunk, chunk)
        for hh in range(group):
            cols = slice(hh * head_dim, (hh + 1) * head_dim)
            qb = q_ref[0, rows, cols]
            kb = k_ref[0, rows, cols]
            vb = v_ref[0, rows, cols]
            scores = lax.dot_general(qb, kb, (((1,), (1,)), ((), ())),
                                     preferred_element_type=F32) * dec_ref[hh]
            out = (jnp.dot(scores.astype(BF16), vb, preferred_element_type=F32)
                   + qd_ref[hh] * jnp.dot(qb, states[hh].astype(BF16),
                                          preferred_element_type=F32))
            kd = (kb.astype(F32) * kd_ref[hh]).astype(BF16)
            kv = lax.dot_general(kd, vb, (((0,), (0,)), ((), ())), preferred_element_type=F32)
            states[hh] = cd_ref[hh] * states[hh] + kv

            mu = jnp.mean(out, axis=-1, keepdims=True)
            cen = out - mu
            var = jnp.mean(cen * cen, axis=-1, keepdims=True)
            o = cen * lax.rsqrt(var + GN_EPS) * gn_ref[:, cols]
            o_ref[0, rows, cols] = (jax.nn.silu(g_ref[0, rows, cols]) * o).astype(o_ref.dtype)
    for hh in range(group):
        state_ref[hh] = states[hh]


def _retention(qkv, rest, gn_g, w_next, *, chunk, rows, group):
    b, s, _ = qkv.shape
    h, d = RET_HEADS, RET_HEAD_DIM
    assert h % group == 0
    hg, gd = h // group, group * d
    nc = s // rows
    wn_rows, wn_cols = w_next.shape
    sr = _slab_rows(wn_rows, b * hg * nc)
    slab = pl.BlockSpec((sr, wn_cols), lambda bi, hi, ci: ((bi * hg + hi) * nc + ci, 0))

    def col(off):
        return pl.BlockSpec((1, rows, gd), lambda bi, hi, ci: (bi, ci, off + hi))

    return pl.pallas_call(
        functools.partial(_retention_kernel, chunk=chunk, n_sub=rows // chunk, head_dim=d),
        grid=(b, hg, nc),
        in_specs=[
            col(0), col(hg), col(2 * hg), col(0),
            pl.BlockSpec((1, gd), lambda bi, hi, ci: (0, hi)),
            slab,
        ],
        out_specs=[pl.BlockSpec((1, rows, gd), lambda bi, hi, ci: (bi, ci, hi)), slab],
        out_shape=[jax.ShapeDtypeStruct((b, s, h * d), BF16),
                   jax.ShapeDtypeStruct((wn_rows, wn_cols), BF16)],
        scratch_shapes=[
            pltpu.VMEM((group, d, d), F32),
            pltpu.VMEM((group, chunk, chunk), F32),
            pltpu.VMEM((group, chunk, d), F32),
            pltpu.VMEM((group, chunk, d), F32),
            pltpu.VMEM((group, 1, d), F32),
        ],
        compiler_params=_params(("parallel", "parallel", "arbitrary")),
        name="retention",
    )(qkv, qkv, qkv, rest, gn_g.reshape(1, h * d), w_next)


def _rglru_kernel(xr_ref, yr_ref, cw_ref, cb_ref, wg_ref, ba_ref, bx_ref, lam_ref, ng_ref,
                  wn_ref, o_ref, wnb_ref, ext_ref, a_ref, u_ref, h_ref, *, tc):
    @pl.when(pl.program_id(1) == 0)
    def _():
        ext_ref[0:SUBLANES, :] = jnp.zeros((SUBLANES, ext_ref.shape[1]), F32)
        h_ref[...] = jnp.zeros_like(h_ref)

    wnb_ref[...] = wn_ref[...].astype(BF16)

    ext_ref[SUBLANES:SUBLANES + tc, :] = xr_ref[0]
    xe = ext_ref[...]
    acc = cw_ref[0:1, :] * xe
    for j in range(1, CONV_WIDTH):
        acc = cw_ref[j:j + 1, :] * xe + pltpu.roll(acc, 1, axis=0)
    xc = acc[SUBLANES:, :] + cb_ref[...]
    ext_ref[0:SUBLANES, :] = ext_ref[tc:tc + SUBLANES, :]

    sp = jax.nn.softplus(-lam_ref[...])
    k_tanh = LRU_C * sp
    k_exp2 = -(LRU_C * LOG2_E) * sp
    for n in range(LRU_BLOCKS):
        sl = slice(n * LRU_BLOCK_DIM, (n + 1) * LRU_BLOCK_DIM)
        xb = xc[:, sl]
        gates = jnp.dot(xb.astype(BF16), wg_ref[n], preferred_element_type=F32)
        r = jax.nn.sigmoid(gates[:, :LRU_BLOCK_DIM] + ba_ref[:, sl])
        i = jax.nn.sigmoid(gates[:, LRU_BLOCK_DIM:] + bx_ref[:, sl])
        a = jnp.exp2(r * k_exp2[:, sl])
        a_ref[n] = a
        one_minus_a2 = jnp.tanh(r * k_tanh[:, sl]) * (1.0 + a * a)
        root = jnp.where(one_minus_a2 > 0.0, one_minus_a2 * lax.rsqrt(one_minus_a2), 0.0)
        u_ref[n] = root * (i * xb)

    sub = lax.broadcasted_iota(jnp.int32, (SUBLANES, LRU_BLOCK_DIM), 0)
    for grp in range(tc // SCAN_ROWS):
        base = grp * SCAN_ROWS
        for n in range(LRU_BLOCKS):
            def seg_rows(jj):
                return pl.ds(base + jj, SUBLANES, stride=SCAN_SEG)

            hs = [u_ref[n, seg_rows(0), :]]
            ps = [a_ref[n, seg_rows(0), :]]
            for jj in range(1, SCAN_SEG):
                a_j = a_ref[n, seg_rows(jj), :]
                hs.append(a_j * hs[-1] + u_ref[n, seg_rows(jj), :])
                ps.append(a_j * ps[-1])
            p_end, h_end = ps[-1], hs[-1]
            shift = 1
            while shift < SUBLANES:
                keep = sub >= shift
                h_prev = pltpu.roll(h_end, shift, axis=0)
                p_prev = pltpu.roll(p_end, shift, axis=0)
                h_end = jnp.where(keep, p_end * h_prev + h_end, h_end)
                p_end = jnp.where(keep, p_end * p_prev, p_end)
                shift *= 2
            h0 = jnp.broadcast_to(h_ref[n, SUBLANES - 1:SUBLANES, :], h_end.shape)
            end_state = h_end + p_end * h0
            h_ref[n] = end_state
            start = jnp.where(sub >= 1, pltpu.roll(end_state, 1, axis=0), h0)
            for jj in range(SCAN_SEG):
                u_ref[n, seg_rows(jj), :] = hs[jj] + ps[jj] * start

    ssq = None
    for n in range(LRU_BLOCKS):
        sl = slice(n * LRU_BLOCK_DIM, (n + 1) * LRU_BLOCK_DIM)
        y = u_ref[n] * yr_ref[0, :, sl]
        u_ref[n] = y
        ssq = y * y if ssq is None else ssq + y * y
    ms = jnp.sum(ssq, axis=-1, keepdims=True) / (LRU_BLOCKS * LRU_BLOCK_DIM)
    scale = jnp.broadcast_to(lax.rsqrt(ms + RMS_EPS), ssq.shape)
    for n in range(LRU_BLOCKS):
        sl = slice(n * LRU_BLOCK_DIM, (n + 1) * LRU_BLOCK_DIM)
        o_ref[0, :, sl] = (u_ref[n] * scale * ng_ref[:, sl]).astype(o_ref.dtype)


def _rglru(proj, conv_w, conv_b, wg, ba, bx, lam, ng, w_next, *, tc):
    b, s, pw = proj.shape
    w = LRU_BLOCKS * LRU_BLOCK_DIM
    xr_blk = (pw - 2 * w) // w
    nt = s // tc
    wn_rows, wn_cols = w_next.shape
    sr = _slab_rows(wn_rows, b * nt)
    slab = pl.BlockSpec((sr, wn_cols), lambda bi, ti: (bi * nt + ti, 0))
    row = lambda a: a.reshape(1, w)
    vec = pl.BlockSpec((1, w), lambda bi, ti: (0, 0))
    return pl.pallas_call(
        functools.partial(_rglru_kernel, tc=tc),
        grid=(b, nt),
        in_specs=[
            pl.BlockSpec((1, tc, w), lambda bi, ti: (bi, ti, xr_blk)),
            pl.BlockSpec((1, tc, w), lambda bi, ti: (bi, ti, xr_blk + 1)),
            pl.BlockSpec((CONV_WIDTH, w), lambda bi, ti: (0, 0)),
            vec,
            pl.BlockSpec((LRU_BLOCKS, LRU_BLOCK_DIM, 2 * LRU_BLOCK_DIM),
                         lambda bi, ti: (0, 0, 0)),
            vec, vec, vec, vec,
            slab,
        ],
        out_specs=[pl.BlockSpec((1, tc, w), lambda bi, ti: (bi, ti, 0)), slab],
        out_shape=[jax.ShapeDtypeStruct((b, s, w), BF16),
                   jax.ShapeDtypeStruct((wn_rows, wn_cols), BF16)],
        scratch_shapes=[
            pltpu.VMEM((tc + SUBLANES, w), F32),
            pltpu.VMEM((LRU_BLOCKS, tc, LRU_BLOCK_DIM), F32),
            pltpu.VMEM((LRU_BLOCKS, tc, LRU_BLOCK_DIM), F32),
            pltpu.VMEM((LRU_BLOCKS, SUBLANES, LRU_BLOCK_DIM), F32),
        ],
        compiler_params=_params(("parallel", "arbitrary")),
        name="rglru",
    )(proj, proj, conv_w, row(conv_b), wg, row(ba), row(bx), row(lam), row(ng), w_next)


def _out_proj_kernel(ret_ref, lru_ref, wr_ref, wl_ref, x_ref, g_ref, o_ref, xg_ref, ssq_ref):
    @pl.when(pl.program_id(1) == 0)
    def _():
        ssq_ref[...] = jnp.zeros_like(ssq_ref)

    acc = jnp.dot(ret_ref[...], wr_ref[...], preferred_element_type=F32)
    acc = acc + jnp.dot(lru_ref[...], wl_ref[...], preferred_element_type=F32)
    x1 = x_ref[...] + acc
    o_ref[...] = x1
    xg_ref[...] = (x1 * g_ref[...]).astype(BF16)
    sq = x1 * x1
    part = sq[:, :LANES]
    for c in range(1, sq.shape[1] // LANES):
        part = part + sq[:, c * LANES:(c + 1) * LANES]
    ssq_ref[...] += part


def _out_proj(ret, lru, w_out, x, g, *, tm, tn):
    t, d = x.shape
    kr, kl = ret.shape[1], lru.shape[1]
    return pl.pallas_call(
        _out_proj_kernel,
        grid=(t // tm, d // tn),
        in_specs=[
            pl.BlockSpec((tm, kr), lambda i, j: (i, 0)),
            pl.BlockSpec((tm, kl), lambda i, j: (i, 0)),
            pl.BlockSpec((kr, tn), lambda i, j: (0, j)),
            pl.BlockSpec((kl, tn), lambda i, j: (kr // kl, j)),
            pl.BlockSpec((tm, tn), lambda i, j: (i, j)),
            pl.BlockSpec((1, tn), lambda i, j: (0, j)),
        ],
        out_specs=[
            pl.BlockSpec((tm, tn), lambda i, j: (i, j)),
            pl.BlockSpec((tm, tn), lambda i, j: (i, j)),
            pl.BlockSpec((tm, LANES), lambda i, j: (i, 0)),
        ],
        out_shape=[
            jax.ShapeDtypeStruct((t, d), F32),
            jax.ShapeDtypeStruct((t, d), BF16),
            jax.ShapeDtypeStruct((t, LANES), F32),
        ],
        compiler_params=_params(("parallel", "arbitrary")),
        name="out_proj",
    )(ret, lru, w_out, w_out, x, g.reshape(1, d))


def _up_kernel(xg_ref, ssq_ref, w_ref, wn_ref, o_ref, wnb_ref, *, d_model):
    ms = jnp.sum(ssq_ref[...], axis=-1, keepdims=True) / d_model
    r = lax.rsqrt(ms + RMS_EPS)
    acc = jnp.dot(xg_ref[...], w_ref[...], preferred_element_type=F32)
    o_ref[...] = jnp.square(jnp.maximum(acc * r, 0.0)).astype(o_ref.dtype)
    wnb_ref[...] = wn_ref[...].astype(BF16)


def _up(xg, ssq, w, w_next, *, tm, tn):
    t, d = xg.shape
    n = w.shape[1]
    ni, nj = t // tm, n // tn
    wn_rows, wn_cols = w_next.shape
    sr = _slab_rows(wn_rows, ni * nj)
    slab = pl.BlockSpec((sr, wn_cols), lambda i, j: (i * nj + j, 0))
    return pl.pallas_call(
        functools.partial(_up_kernel, d_model=d),
        grid=(ni, nj),
        in_specs=[
            pl.BlockSpec((tm, d), lambda i, j: (i, 0)),
            pl.BlockSpec((tm, LANES), lambda i, j: (i, 0)),
            pl.BlockSpec((d, tn), lambda i, j: (0, j)),
            slab,
        ],
        out_specs=[pl.BlockSpec((tm, tn), lambda i, j: (i, j)), slab],
        out_shape=[jax.ShapeDtypeStruct((t, n), BF16),
                   jax.ShapeDtypeStruct((wn_rows, wn_cols), BF16)],
        compiler_params=_params(("parallel", "arbitrary")),
        name="mlp_up",
    )(xg, ssq, w, w_next)


def _down_norm_kernel(a_ref, w_ref, x_ref, g_ref, o_ref, r_ref, *, tn):
    kk = pl.program_id(1)
    d = o_ref.shape[1]
    xs = x_ref.shape[1]

    last = pl.num_programs(1) - 1
    lanes = pl.ds(pl.multiple_of(kk * xs, xs), xs)

    def panels(first, final):
        if not first:
            o_ref[:, lanes] += x_ref[...]
        ssq = None
        for c in range(d // tn):
            cols = slice(c * tn, (c + 1) * tn)
            val = jnp.dot(a_ref[...], w_ref[:, cols], preferred_element_type=F32)
            if not first:
                val = o_ref[:, cols] + val
            o_ref[:, cols] = val
            if final:
                sq = val * val
                for l in range(tn // LANES):
                    blk = sq[:, l * LANES:(l + 1) * LANES]
                    ssq = blk if ssq is None else ssq + blk
        if first:
            o_ref[:, lanes] += x_ref[...]
        if final:
            ms = jnp.sum(ssq, axis=-1, keepdims=True) / d
            r_ref[...] = jnp.broadcast_to(lax.rsqrt(ms + RMS_EPS), r_ref.shape)

    pl.when(kk == 0)(functools.partial(panels, True, False))
    pl.when(jnp.logical_and(kk != 0, kk != last))(functools.partial(panels, False, False))
    pl.when(kk == last)(functools.partial(panels, False, True))

    @pl.when(kk == last)
    def _():
        def apply_scale(r, carry):
            rows = pl.ds(pl.multiple_of(r * NORM_ROWS, NORM_ROWS), NORM_ROWS)
            scale = r_ref[rows, :]
            for l in range(d // LANES):
                cols = slice(l * LANES, (l + 1) * LANES)
                o_ref[rows, cols] = o_ref[rows, cols] * scale * g_ref[:, cols]
            return carry

        lax.fori_loop(0, o_ref.shape[0] // NORM_ROWS, apply_scale, 0, unroll=2)


def _down_norm(act, w_down, x, g, *, tm, tk, tn):
    t, d = x.shape
    f = act.shape[1]
    nk = f // tk
    assert nk > 1 and d % nk == 0 and (d // nk) % LANES == 0
    return pl.pallas_call(
        functools.partial(_down_norm_kernel, tn=tn),
        grid=(t // tm, nk),
        in_specs=[
            pl.BlockSpec((tm, tk), lambda i, k: (i, k)),
            pl.BlockSpec((tk, d), lambda i, k: (k, 0)),
            pl.BlockSpec((tm, d // nk), lambda i, k: (i, k)),
            pl.BlockSpec((1, d), lambda i, k: (0, 0)),
        ],
        out_specs=pl.BlockSpec((tm, d), lambda i, k: (i, 0)),
        out_shape=jax.ShapeDtypeStruct((t, d), F32),
        scratch_shapes=[pltpu.VMEM((tm, LANES), F32)],
        compiler_params=_params(("parallel", "arbitrary")),
        name="down_norm",
    )(act, w_down, x, g.reshape(1, d))


def kernel(x, norm1_g, w_in, ret_gn_g, conv_w, conv_b, gate_a_w, gate_a_b, gate_x_w, gate_x_b,
           lru_lambda, lru_norm_g, w_out, norm2_g, w_up, w_down, normf_g):
    b, s, d = x.shape
    assert w_in.shape[0] == 1, "single-layer problem"
    cos, sin = _rope_table(s, RET_HEAD_DIM // 2)
    xt = x.reshape(b * s, d)
    ret_width = RET_HEADS * RET_HEAD_DIM
    qkv, rest = _in_proj(xt, norm1_g[0], w_in[0].astype(BF16), cos, sin, tm=1024, tn=1024,
                         n_norm=8, ret_width=ret_width,
                         lru_width=LRU_BLOCKS * LRU_BLOCK_DIM, head_dim=RET_HEAD_DIM)
    qkv = qkv.reshape(b, s, -1)
    rest = rest.reshape(b, s, -1)
    ret, w_out_b = _retention(qkv, rest, ret_gn_g[0], w_out[0], chunk=256, rows=2048, group=2)
    wg = jnp.concatenate([gate_a_w[0], gate_x_w[0]], axis=-1).astype(BF16)
    lru, w_up_b = _rglru(rest, conv_w[0], conv_b[0], wg, gate_a_b[0], gate_x_b[0],
                         lru_lambda[0], lru_norm_g[0], w_up[0], tc=256)
    x1, xg, ssq = _out_proj(ret.reshape(b * s, -1), lru.reshape(b * s, -1),
                            w_out_b, xt, norm2_g[0], tm=1024, tn=512)
    act, w_down_b = _up(xg, ssq, w_up_b, w_down[0], tm=1024, tn=1024)
    out = _down_norm(act, w_down_b, x1, normf_g, tm=1024, tk=1024, tn=1024)
    return out.reshape(b, s, d)
```

```python
import functools
import math

import jax
import jax.numpy as jnp
from jax import lax
from jax.experimental import pallas as pl
from jax.experimental.pallas import tpu as pltpu

RET_HEADS = 8
RET_HEAD_DIM = 256
LRU_BLOCKS = 16
LRU_BLOCK_DIM = 128
CONV_WIDTH = 4
ROPE_BASE = 10000.0
LRU_C = 8.0
LOG2_E = math.log2(math.e)
RMS_EPS = 1e-6
GN_EPS = 1e-5

SUBLANES = 8
LANES = 128
V7X_VMEM_LIMIT_BYTES = 60 * 1024 * 1024
NORM_ROWS = 4 * SUBLANES
SCAN_SEG = 4
SCAN_ROWS = SCAN_SEG * SUBLANES

F32 = jnp.float32
BF16 = jnp.bfloat16


def _params(semantics):
    return pltpu.CompilerParams(dimension_semantics=semantics,
                                vmem_limit_bytes=V7X_VMEM_LIMIT_BYTES)


def _rope_kernel(cos_ref, sin_ref, *, rows, half):
    r0 = pl.program_id(0) * rows
    pos = (lax.broadcasted_iota(jnp.int32, (rows, half), 0) + r0).astype(F32)
    idx = lax.broadcasted_iota(jnp.int32, (rows, half), 1).astype(F32)
    inv = jnp.exp(-(idx / half) * jnp.log(F32(ROPE_BASE)))
    ang = pos * inv
    cos_ref[...] = jnp.cos(ang)
    sin_ref[...] = jnp.sin(ang)


def _rope_table(seq, half, rows=512):
    return pl.pallas_call(
        functools.partial(_rope_kernel, rows=rows, half=half),
        grid=(seq // rows,),
        out_specs=[pl.BlockSpec((rows, half), lambda i: (i, 0))] * 2,
        out_shape=[jax.ShapeDtypeStruct((seq, half), F32)] * 2,
        compiler_params=_params(("parallel",)),
        name="rope_table",
    )()


def _in_proj_kernel(x_ref, g_ref, w_ref, cos_ref, sin_ref, qkv_ref, rest_ref, h_ref, *,
                    n_norm, bounds, head_dim):
    i = pl.program_id(0)
    j = pl.program_id(1)
    half = head_dim // 2
    chunk = x_ref.shape[0]
    parity = i % 2

    def norm_ahead():
        x = x_ref[...]
        ms = jnp.mean(x * x, axis=-1, keepdims=True)
        rows = pl.ds(pl.multiple_of(j * chunk, chunk), chunk)
        h_ref[parity, rows, :] = (x * lax.rsqrt(ms + RMS_EPS) * g_ref[...]).astype(BF16)

    def tile(kind, with_norm):
        acc = jnp.dot(h_ref[1 - parity], w_ref[...], preferred_element_type=F32)
        if kind == "rotary":
            cos = cos_ref[...]
            sin = sin_ref[...]
            for hd in range(acc.shape[1] // head_dim):
                lo = slice(hd * head_dim, hd * head_dim + half)
                hi = slice(hd * head_dim + half, (hd + 1) * head_dim)
                t1, t2 = acc[:, lo], acc[:, hi]
                qkv_ref[:, lo] = (t1 * cos - t2 * sin).astype(qkv_ref.dtype)
                qkv_ref[:, hi] = (t2 * cos + t1 * sin).astype(qkv_ref.dtype)
        elif kind == "bf16":
            qkv_ref[...] = acc.astype(qkv_ref.dtype)
        elif kind == "gelu":
            rest_ref[...] = jax.nn.gelu(acc, approximate=True)
        else:
            rest_ref[...] = acc
        if with_norm:
            norm_ahead()

    pl.when(jnp.logical_and(i == 0, j < n_norm))(norm_ahead)

    kinds = ("rotary", "bf16", "f32", "gelu")
    for kind, lo, hi in zip(kinds, bounds[:-1], bounds[1:]):
        for with_norm, a, b in ((True, lo, min(hi, n_norm)), (False, max(lo, n_norm), hi)):
            if a < b:
                cond = jnp.logical_and(i > 0, jnp.logical_and(j >= a, j < b))
                pl.when(cond)(functools.partial(tile, kind, with_norm))


def _in_proj(x, g, w, cos, sin, *, tm, tn, n_norm, ret_width, lru_width, head_dim):
    t, d = x.shape
    n = w.shape[1]
    seq = cos.shape[0]
    assert n == 4 * ret_width + 2 * lru_width
    assert ret_width % tn == 0 and lru_width % tn == 0 and tn % head_dim == 0
    assert seq % tm == 0
    rw, lw = ret_width // tn, lru_width // tn
    bounds = (0, 2 * rw, 3 * rw, 4 * rw + lw, 4 * rw + 2 * lw)
    n_bf16 = 3 * rw
    bf16_width = 3 * ret_width
    ni = t // tm
    pos_blocks = seq // tm
    chunk = tm // n_norm
    assert n_norm <= n // tn and tm % n_norm == 0 and chunk % (2 * SUBLANES) == 0

    def mat_block(i):
        return jnp.maximum(i - 1, 0)

    def x_index(i, j):
        return jnp.minimum(i, ni - 1) * n_norm + jnp.minimum(j, n_norm - 1), 0

    def first(i, blk):
        return jnp.where(i == 0, 0, blk)

    return pl.pallas_call(
        functools.partial(_in_proj_kernel, n_norm=n_norm, bounds=bounds, head_dim=head_dim),
        grid=(ni + 1, n // tn),
        in_specs=[
            pl.BlockSpec((chunk, d), x_index),
            pl.BlockSpec((1, d), lambda i, j: (0, 0)),
            pl.BlockSpec((d, tn), lambda i, j: (0, first(i, j))),
            pl.BlockSpec((tm, head_dim // 2), lambda i, j: (mat_block(i) % pos_blocks, 0)),
            pl.BlockSpec((tm, head_dim // 2), lambda i, j: (mat_block(i) % pos_blocks, 0)),
        ],
        out_specs=[
            pl.BlockSpec((tm, tn),
                         lambda i, j: (mat_block(i), first(i, jnp.minimum(j, n_bf16 - 1)))),
            pl.BlockSpec((tm, tn),
                         lambda i, j: (mat_block(i), first(i, jnp.maximum(j - n_bf16, 0)))),
        ],
        out_shape=[jax.ShapeDtypeStruct((t, bf16_width), BF16),
                   jax.ShapeDtypeStruct((t, n - bf16_width), F32)],
        scratch_shapes=[pltpu.VMEM((2, tm, d), BF16)],
        compiler_params=_params(("arbitrary", "arbitrary")),
        name="in_proj",
    )(x, g.reshape(1, d), w, cos, sin)


def _slab_rows(w_rows, steps):
    rows, rem = divmod(w_rows, steps)
    assert rem == 0 and rows % (2 * SUBLANES) == 0, (w_rows, steps)
    return rows


def _retention_kernel(q_ref, k_ref, v_ref, g_ref, gn_ref, wn_ref, o_ref, wnb_ref,
                      state_ref, dec_ref, qd_ref, kd_ref, cd_ref, *, chunk, n_sub, head_dim):
    scale = head_dim ** -0.5
    assert math.frexp(scale)[0] == 0.5
    group = state_ref.shape[0]

    @pl.when(pl.program_id(2) == 0)
    def _():
        state_ref[...] = jnp.zeros_like(state_ref)
        row = lax.broadcasted_iota(jnp.int32, (chunk, chunk), 0)
        col = lax.broadcasted_iota(jnp.int32, (chunk, chunk), 1)
        diff = (row - col).astype(F32)
        idx = lax.broadcasted_iota(jnp.int32, (chunk, head_dim), 0).astype(F32)
        for hh in range(group):
            head = pl.program_id(1) * group + hh
            hf = jnp.full((1, 1), head, jnp.int32).astype(F32)
            log_g = jnp.log1p(-jnp.exp2(-5.0 - hf))
            dec_ref[hh] = jnp.where(diff >= 0, jnp.exp(log_g * jnp.maximum(diff, 0.0)),
                                    0.0) * scale
            qd_ref[hh] = jnp.exp(log_g * (idx + 1.0))
            kd_ref[hh] = jnp.exp(log_g * (chunk - 1.0 - idx)) * scale
            cd_ref[hh] = jnp.broadcast_to(jnp.exp(log_g * chunk), cd_ref.shape[1:])

    wnb_ref[...] = wn_ref[...].astype(BF16)

    states = [state_ref[hh] for hh in range(group)]
    for c in range(n_sub):
        rows = pl.ds(c * chunk, chunk)
        for hh in range(group):
            cols = slice(hh * head_dim, (hh + 1) * head_dim)
            qb = q_ref[0, rows, cols]
            kb = k_ref[0, rows, cols]
            vb = v_ref[0, rows, cols]
            scores = lax.dot_general(qb, kb, (((1,), (1,)), ((), ())),
                                     preferred_element_type=F32) * dec_ref[hh]
            out = (jnp.dot(scores.astype(BF16), vb, preferred_element_type=F32)
                   + qd_ref[hh] * jnp.dot(qb, states[hh].astype(BF16),
                                          preferred_element_type=F32))
            kd = (kb.astype(F32) * kd_ref[hh]).astype(BF16)
            kv = lax.dot_general(kd, vb, (((0,), (0,)), ((), ())), preferred_element_type=F32)
            states[hh] = cd_ref[hh] * states[hh] + kv

            mu = jnp.mean(out, axis=-1, keepdims=True)
            cen = out - mu
            var = jnp.mean(cen * cen, axis=-1, keepdims=True)
            o = cen * lax.rsqrt(var + GN_EPS) * gn_ref[:, cols]
            o_ref[0, rows, cols] = (jax.nn.silu(g_ref[0, rows, cols]) * o).astype(o_ref.dtype)
    for hh in range(group):
        state_ref[hh] = states[hh]


def _retention(qkv, rest, gn_g, w_next, *, chunk, rows, group):
    b, s, _ = qkv.shape
    h, d = RET_HEADS, RET_HEAD_DIM
    assert h % group == 0
    hg, gd = h // group, group * d
    nc = s // rows
    wn_rows, wn_cols = w_next.shape
    sr = _slab_rows(wn_rows, b * hg * nc)
    slab = pl.BlockSpec((sr, wn_cols), lambda bi, hi, ci: ((bi * hg + hi) * nc + ci, 0))

    def col(off):
        return pl.BlockSpec((1, rows, gd), lambda bi, hi, ci: (bi, ci, off + hi))

    return pl.pallas_call(
        functools.partial(_retention_kernel, chunk=chunk, n_sub=rows // chunk, head_dim=d),
        grid=(b, hg, nc),
        in_specs=[
            col(0), col(hg), col(2 * hg), col(0),
            pl.BlockSpec((1, gd), lambda bi, hi, ci: (0, hi)),
            slab,
        ],
        out_specs=[pl.BlockSpec((1, rows, gd), lambda bi, hi, ci: (bi, ci, hi)), slab],
        out_shape=[jax.ShapeDtypeStruct((b, s, h * d), BF16),
                   jax.ShapeDtypeStruct((wn_rows, wn_cols), BF16)],
        scratch_shapes=[
            pltpu.VMEM((group, d, d), F32),
            pltpu.VMEM((group, chunk, chunk), F32),
            pltpu.VMEM((group, chunk, d), F32),
            pltpu.VMEM((group, chunk, d), F32),
            pltpu.VMEM((group, 1, d), F32),
        ],
        compiler_params=_params(("parallel", "parallel", "arbitrary")),
        name="retention",
    )(qkv, qkv, qkv, rest, gn_g.reshape(1, h * d), w_next)


def _rglru_kernel(xr_ref, yr_ref, cw_ref, cb_ref, wg_ref, ba_ref, bx_ref, lam_ref, ng_ref,
                  wn_ref, o_ref, wnb_ref, ext_ref, a_ref, u_ref, h_ref, *, tc):
    @pl.when(pl.program_id(1) == 0)
    def _():
        ext_ref[0:SUBLANES, :] = jnp.zeros((SUBLANES, ext_ref.shape[1]), F32)
        h_ref[...] = jnp.zeros_like(h_ref)

    wnb_ref[...] = wn_ref[...].astype(BF16)

    ext_ref[SUBLANES:SUBLANES + tc, :] = xr_ref[0]
    xe = ext_ref[...]
    acc = cw_ref[0:1, :] * xe
    for j in range(1, CONV_WIDTH):
        acc = cw_ref[j:j + 1, :] * xe + pltpu.roll(acc, 1, axis=0)
    xc = acc[SUBLANES:, :] + cb_ref[...]
    ext_ref[0:SUBLANES, :] = ext_ref[tc:tc + SUBLANES, :]

    sp = jax.nn.softplus(-lam_ref[...])
    k_tanh = LRU_C * sp
    k_exp2 = -(LRU_C * LOG2_E) * sp
    for n in range(LRU_BLOCKS):
        sl = slice(n * LRU_BLOCK_DIM, (n + 1) * LRU_BLOCK_DIM)
        xb = xc[:, sl]
        gates = jnp.dot(xb.astype(BF16), wg_ref[n], preferred_element_type=F32)
        r = jax.nn.sigmoid(gates[:, :LRU_BLOCK_DIM] + ba_ref[:, sl])
        i = jax.nn.sigmoid(gates[:, LRU_BLOCK_DIM:] + bx_ref[:, sl])
        a = jnp.exp2(r * k_exp2[:, sl])
        a_ref[n] = a
        one_minus_a2 = jnp.tanh(r * k_tanh[:, sl]) * (1.0 + a * a)
        root = jnp.where(one_minus_a2 > 0.0, one_minus_a2 * lax.rsqrt(one_minus_a2), 0.0)
        u_ref[n] = root * (i * xb)

    sub = lax.broadcasted_iota(jnp.int32, (SUBLANES, LRU_BLOCK_DIM), 0)
    for grp in range(tc // SCAN_ROWS):
        base = grp * SCAN_ROWS
        for n in range(LRU_BLOCKS):
            def seg_rows(jj):
                return pl.ds(base + jj, SUBLANES, stride=SCAN_SEG)

            hs = [u_ref[n, seg_rows(0), :]]
            ps = [a_ref[n, seg_rows(0), :]]
            for jj in range(1, SCAN_SEG):
                a_j = a_ref[n, seg_rows(jj), :]
                hs.append(a_j * hs[-1] + u_ref[n, seg_rows(jj), :])
                ps.append(a_j * ps[-1])
            p_end, h_end = ps[-1], hs[-1]
            shift = 1
            while shift < SUBLANES:
                keep = sub >= shift
                h_prev = pltpu.roll(h_end, shift, axis=0)
                p_prev = pltpu.roll(p_end, shift, axis=0)
                h_end = jnp.where(keep, p_end * h_prev + h_end, h_end)
                p_end = jnp.where(keep, p_end * p_prev, p_end)
                shift *= 2
            h0 = jnp.broadcast_to(h_ref[n, SUBLANES - 1:SUBLANES, :], h_end.shape)
            end_state = h_end + p_end * h0
            h_ref[n] = end_state
            start = jnp.where(sub >= 1, pltpu.roll(end_state, 1, axis=0), h0)
            for jj in range(SCAN_SEG):
                u_ref[n, seg_rows(jj), :] = hs[jj] + ps[jj] * start

    ssq = None
    for n in range(LRU_BLOCKS):
        sl = slice(n * LRU_BLOCK_DIM, (n + 1) * LRU_BLOCK_DIM)
        y = u_ref[n] * yr_ref[0, :, sl]
        u_ref[n] = y
        ssq = y * y if ssq is None else ssq + y * y
    ms = jnp.sum(ssq, axis=-1, keepdims=True) / (LRU_BLOCKS * LRU_BLOCK_DIM)
    scale = jnp.broadcast_to(lax.rsqrt(ms + RMS_EPS), ssq.shape)
    for n in range(LRU_BLOCKS):
        sl = slice(n * LRU_BLOCK_DIM, (n + 1) * LRU_BLOCK_DIM)
        o_ref[0, :, sl] = (u_ref[n] * scale * ng_ref[:, sl]).astype(o_ref.dtype)


def _rglru(proj, conv_w, conv_b, wg, ba, bx, lam, ng, w_next, *, tc):
    b, s, pw = proj.shape
    w = LRU_BLOCKS * LRU_BLOCK_DIM
    xr_blk = (pw - 2 * w) // w
    nt = s // tc
    wn_rows, wn_cols = w_next.shape
    sr = _slab_rows(wn_rows, b * nt)
    slab = pl.BlockSpec((sr, wn_cols), lambda bi, ti: (bi * nt + ti, 0))
    row = lambda a: a.reshape(1, w)
    vec = pl.BlockSpec((1, w), lambda bi, ti: (0, 0))
    return pl.pallas_call(
        functools.partial(_rglru_kernel, tc=tc),
        grid=(b, nt),
        in_specs=[
            pl.BlockSpec((1, tc, w), lambda bi, ti: (bi, ti, xr_blk)),
            pl.BlockSpec((1, tc, w), lambda bi, ti: (bi, ti, xr_blk + 1)),
            pl.BlockSpec((CONV_WIDTH, w), lambda bi, ti: (0, 0)),
            vec,
            pl.BlockSpec((LRU_BLOCKS, LRU_BLOCK_DIM, 2 * LRU_BLOCK_DIM),
                         lambda bi, ti: (0, 0, 0)),
            vec, vec, vec, vec,
            slab,
        ],
        out_specs=[pl.BlockSpec((1, tc, w), lambda bi, ti: (bi, ti, 0)), slab],
        out_shape=[jax.ShapeDtypeStruct((b, s, w), BF16),
                   jax.ShapeDtypeStruct((wn_rows, wn_cols), BF16)],
        scratch_shapes=[
            pltpu.VMEM((tc + SUBLANES, w), F32),
            pltpu.VMEM((LRU_BLOCKS, tc, LRU_BLOCK_DIM), F32),
            pltpu.VMEM((LRU_BLOCKS, tc, LRU_BLOCK_DIM), F32),
            pltpu.VMEM((LRU_BLOCKS, SUBLANES, LRU_BLOCK_DIM), F32),
        ],
        compiler_params=_params(("parallel", "arbitrary")),
        name="rglru",
    )(proj, proj, conv_w, row(conv_b), wg, row(ba), row(bx), row(lam), row(ng), w_next)


def _out_proj_kernel(ret_ref, lru_ref, wr_ref, wl_ref, x_ref, g_ref, o_ref, xg_ref, ssq_ref):
    acc = jnp.dot(ret_ref[...], wr_ref[...], preferred_element_type=F32)
    acc = acc + jnp.dot(lru_ref[...], wl_ref[...], preferred_element_type=F32)
    x1 = x_ref[...] + acc
    o_ref[...] = x1
    xg_ref[...] = (x1 * g_ref[...]).astype(BF16)
    sq = x1 * x1
    part = sq[:, :LANES]
    for c in range(1, sq.shape[1] // LANES):
        part = part + sq[:, c * LANES:(c + 1) * LANES]
    ssq_ref[...] = part


def _out_proj(ret, lru, w_out, x, g, *, tm, tn):
    t, d = x.shape
    kr, kl = ret.shape[1], lru.shape[1]
    nj = d // tn
    any_spec = pl.BlockSpec(memory_space=pl.ANY)
    stream3 = pl.Buffered(3)

    def body(*refs):
        pltpu.emit_pipeline(
            _out_proj_kernel,
            grid=(t // tm, nj),
            in_specs=[
                pl.BlockSpec((tm, kr), lambda i, j: (i, 0)),
                pl.BlockSpec((tm, kl), lambda i, j: (i, 0)),
                pl.BlockSpec((kr, tn), lambda i, j: (0, j), pipeline_mode=stream3),
                pl.BlockSpec((kl, tn), lambda i, j: (kr // kl, j), pipeline_mode=stream3),
                pl.BlockSpec((tm, tn), lambda i, j: (i, j), pipeline_mode=stream3),
                pl.BlockSpec((1, tn), lambda i, j: (0, j)),
            ],
            out_specs=[
                pl.BlockSpec((tm, tn), lambda i, j: (i, j)),
                pl.BlockSpec((tm, tn), lambda i, j: (i, j)),
                pl.BlockSpec((tm, LANES), lambda i, j: (i, j)),
            ],
        )(*refs)

    return pl.pallas_call(
        body,
        in_specs=[any_spec] * 6,
        out_specs=[any_spec] * 3,
        out_shape=[
            jax.ShapeDtypeStruct((t, d), F32),
            jax.ShapeDtypeStruct((t, d), BF16),
            jax.ShapeDtypeStruct((t, nj * LANES), F32),
        ],
        compiler_params=pltpu.CompilerParams(vmem_limit_bytes=V7X_VMEM_LIMIT_BYTES),
        name="out_proj",
    )(ret, lru, w_out, w_out, x, g.reshape(1, d))


def _up_kernel(xg_ref, ssq_ref, w_ref, wn_ref, o_ref, wnb_ref, *, d_model):
    ms = jnp.sum(ssq_ref[...], axis=-1, keepdims=True) / d_model
    r = lax.rsqrt(ms + RMS_EPS)
    acc = jnp.dot(xg_ref[...], w_ref[...], preferred_element_type=F32)
    o_ref[...] = jnp.square(jnp.maximum(acc * r, 0.0)).astype(o_ref.dtype)
    wnb_ref[...] = wn_ref[...].astype(BF16)


def _up(xg, ssq, w, w_next, *, tm, tn):
    t, d = xg.shape
    n = w.shape[1]
    ni, nj = t // tm, n // tn
    wn_rows, wn_cols = w_next.shape
    sr = _slab_rows(wn_rows, ni * nj)
    slab = pl.BlockSpec((sr, wn_cols), lambda i, j: (i * nj + j, 0))
    return pl.pallas_call(
        functools.partial(_up_kernel, d_model=d),
        grid=(ni, nj),
        in_specs=[
            pl.BlockSpec((tm, d), lambda i, j: (i, 0)),
            pl.BlockSpec((tm, ssq.shape[1]), lambda i, j: (i, 0)),
            pl.BlockSpec((d, tn), lambda i, j: (0, j)),
            slab,
        ],
        out_specs=[pl.BlockSpec((tm, tn), lambda i, j: (i, j)), slab],
        out_shape=[jax.ShapeDtypeStruct((t, n), BF16),
                   jax.ShapeDtypeStruct((wn_rows, wn_cols), BF16)],
        compiler_params=_params(("parallel", "arbitrary")),
        name="mlp_up",
    )(xg, ssq, w, w_next)


def _down_norm_kernel(a_ref, w_ref, x_ref, g_ref, o_ref, r_ref, *, tn):
    kk = pl.program_id(1)
    d = o_ref.shape[1]
    xs = x_ref.shape[1]

    last = pl.num_programs(1) - 1
    lanes = pl.ds(pl.multiple_of(kk * xs, xs), xs)

    def panels(first, final):
        if not first:
            o_ref[:, lanes] += x_ref[...]
        ssq = None
        for c in range(d // tn):
            cols = slice(c * tn, (c + 1) * tn)
            val = jnp.dot(a_ref[...], w_ref[:, cols], preferred_element_type=F32)
            if not first:
                val = o_ref[:, cols] + val
            o_ref[:, cols] = val
            if final:
                sq = val * val
                for l in range(tn // LANES):
                    blk = sq[:, l * LANES:(l + 1) * LANES]
                    ssq = blk if ssq is None else ssq + blk
        if first:
            o_ref[:, lanes] += x_ref[...]
        if final:
            ms = jnp.sum(ssq, axis=-1, keepdims=True) / d
            r_ref[...] = jnp.broadcast_to(lax.rsqrt(ms + RMS_EPS), r_ref.shape)

    pl.when(kk == 0)(functools.partial(panels, True, False))
    pl.when(jnp.logical_and(kk != 0, kk != last))(functools.partial(panels, False, False))
    pl.when(kk == last)(functools.partial(panels, False, True))

    @pl.when(kk == last)
    def _():
        def apply_scale(r, carry):
            rows = pl.ds(pl.multiple_of(r * NORM_ROWS, NORM_ROWS), NORM_ROWS)
            scale = r_ref[rows, :]
            for l in range(d // LANES):
                cols = slice(l * LANES, (l + 1) * LANES)
                o_ref[rows, cols] = o_ref[rows, cols] * scale * g_ref[:, cols]
            return carry

        lax.fori_loop(0, o_ref.shape[0] // NORM_ROWS, apply_scale, 0, unroll=2)


def _down_norm(act, w_down, x, g, *, tm, tk, tn):
    t, d = x.shape
    f = act.shape[1]
    nk = f // tk
    assert nk > 1 and d % nk == 0 and (d // nk) % LANES == 0
    return pl.pallas_call(
        functools.partial(_down_norm_kernel, tn=tn),
        grid=(t // tm, nk),
        in_specs=[
            pl.BlockSpec((tm, tk), lambda i, k: (i, k)),
            pl.BlockSpec((tk, d), lambda i, k: (k, 0)),
            pl.BlockSpec((tm, d // nk), lambda i, k: (i, k)),
            pl.BlockSpec((1, d), lambda i, k: (0, 0)),
        ],
        out_specs=pl.BlockSpec((tm, d), lambda i, k: (i, 0)),
        out_shape=jax.ShapeDtypeStruct((t, d), F32),
        scratch_shapes=[pltpu.VMEM((tm, LANES), F32)],
        compiler_params=_params(("parallel", "arbitrary")),
        name="down_norm",
    )(act, w_down, x, g.reshape(1, d))


def kernel(x, norm1_g, w_in, ret_gn_g, conv_w, conv_b, gate_a_w, gate_a_b, gate_x_w, gate_x_b,
           lru_lambda, lru_norm_g, w_out, norm2_g, w_up, w_down, normf_g):
    b, s, d = x.shape
    assert w_in.shape[0] == 1, "single-layer problem"
    cos, sin = _rope_table(s, RET_HEAD_DIM // 2)
    xt = x.reshape(b * s, d)
    ret_width = RET_HEADS * RET_HEAD_DIM
    qkv, rest = _in_proj(xt, norm1_g[0], w_in[0].astype(BF16), cos, sin, tm=1024, tn=1024,
                         n_norm=8, ret_width=ret_width,
                         lru_width=LRU_BLOCKS * LRU_BLOCK_DIM, head_dim=RET_HEAD_DIM)
    qkv = qkv.reshape(b, s, -1)
    rest = rest.reshape(b, s, -1)
    ret, w_out_b = _retention(qkv, rest, ret_gn_g[0], w_out[0], chunk=256, rows=2048, group=2)
    wg = jnp.concatenate([gate_a_w[0], gate_x_w[0]], axis=-1).astype(BF16)
    lru, w_up_b = _rglru(rest, conv_w[0], conv_b[0], wg, gate_a_b[0], gate_x_b[0],
                         lru_lambda[0], lru_norm_g[0], w_up[0], tc=256)
    x1, xg, ssq = _out_proj(ret.reshape(b * s, -1), lru.reshape(b * s, -1),
                            w_out_b, xt, norm2_g[0], tm=1024, tn=512)
    act, w_down_b = _up(xg, ssq, w_up_b, w_down[0], tm=1024, tn=1024)
    out = _down_norm(act, w_down_b, x1, normf_g, tm=1024, tk=1024, tn=1024)
    return out.reshape(b, s, d)
```
